```python
import math
import jax, jax.numpy as jnp
from jax import lax
import numpy as np

D_MODEL = 1024
BATCH = 4
SEQ = 8192
DEPTH = 1
DEC_BATCH = 128
DEC_SEQ = 8
PAST_LEN = 8192
PAGE_SIZE = 128

N_HEADS = 8
HEAD_DIM = 64
N_KV_HEADS = 4
GQA_GROUP = N_HEADS // N_KV_HEADS
V_DIM = 2 * HEAD_DIM
DIFF_WIDTH = N_HEADS * V_DIM
Q_BLOCK = 128
POOL_WINDOWS = (2, 4, 8, 16)
POOL_GROUPS = 4
POOL_WIDTH = D_MODEL
POOL_GROUP_DIM = POOL_WIDTH // POOL_GROUPS
POOL_STATE = 15
N_MEM = 256
X_HEADS = 4
X_DIM = D_MODEL // X_HEADS
X_WIDTH = X_HEADS * X_DIM
N_BRANCH = 3
N_EXPERTS = 32
TOP_K = 4
D_FF = D_MODEL
SWIGLU_LIMIT = 7.0
SWIGLU_ALPHA = 1.702
MOE_BLOCK = 128
RMS_EPS = 1e-6
SPLIT_WIDTHS = (POOL_WIDTH, N_HEADS * 2 * HEAD_DIM, N_KV_HEADS * 2 * HEAD_DIM, N_KV_HEADS * V_DIM, X_WIDTH, N_BRANCH * D_MODEL)
IN_WIDTH = POOL_WIDTH + N_HEADS * 2 * HEAD_DIM + N_KV_HEADS * 2 * HEAD_DIM + N_KV_HEADS * V_DIM + X_WIDTH + N_BRANCH * D_MODEL

kernel_name = 'hybrid_pool_diffattn_xmem_moe_step'


def _rms(x, g):
    xf = x.astype(jnp.float32)
    y = xf * lax.rsqrt(jnp.mean(xf * xf, axis=-1, keepdims=True) + RMS_EPS) * g.astype(jnp.float32)
    return y.astype(x.dtype)


def _split(z):
    outs, off = [], 0
    for w in SPLIT_WIDTHS:
        outs.append(z[..., off:off + w])
        off += w
    return outs


def _pool_branch(u, hist, w_group, scale):
    b, t, _ = u.shape
    n_prev = hist.shape[1]
    ue = jnp.concatenate([hist, u], axis=1).astype(jnp.float32)
    cs = jnp.concatenate([jnp.zeros((b, 1, POOL_WIDTH), jnp.float32), jnp.cumsum(ue, axis=1)], axis=1)
    cs = cs.reshape(b, n_prev + t + 1, POOL_GROUPS, POOL_GROUP_DIM)
    hi = n_prev + 1 + np.arange(t)
    means = []
    for g, w in enumerate(POOL_WINDOWS):
        lo = np.maximum(hi - w, 0)
        cnt = (hi - lo).astype(np.float32)
        csg = cs[:, :, g]
        means.append((jnp.take(csg, hi, axis=1) - jnp.take(csg, lo, axis=1)) / cnt[None, :, None])
    pooled = jnp.stack(means, axis=2) - ue[:, n_prev:].reshape(b, t, POOL_GROUPS, POOL_GROUP_DIM)
    y = jnp.einsum('btgc,gcd->btgd', pooled, w_group.astype(jnp.float32))
    return y.reshape(b, t, POOL_WIDTH) * scale.astype(jnp.float32)


def _diff_weights(p, lam):
    return p[:, :, :, 0] - lam * p[:, :, :, 1]


def _diff_prompt(q, k, v, lam):
    b, s = q.shape[:2]
    nb = s // Q_BLOCK
    qb = q.reshape(b, nb, Q_BLOCK, N_KV_HEADS, GQA_GROUP, 2, HEAD_DIM).swapaxes(0, 1)
    k_pos = jnp.arange(s)
    def one(args):
        qi, i = args
        sc = jnp.einsum('bqhgmd,bkhmd->bhgmqk', qi, k).astype(jnp.float32) * (HEAD_DIM ** -0.5)
        q_pos = i * Q_BLOCK + jnp.arange(Q_BLOCK)
        sc = jnp.where(k_pos[None, :] <= q_pos[:, None], sc, -jnp.inf)
        a = _diff_weights(jax.nn.softmax(sc, axis=-1), lam)
        return jnp.einsum('bhgqk,bkhv->bqhgv', a, v.astype(jnp.float32))
    o = lax.map(one, (qb, jnp.arange(nb)))
    return o.swapaxes(0, 1).reshape(b, s, N_KV_HEADS, GQA_GROUP, V_DIM)


def _diff_sample(q, k, v, k_past, v_past, lam):
    t = q.shape[1]
    n_past = k_past.shape[1]
    scale = HEAD_DIM ** -0.5
    s_p = jnp.einsum('bqhgmd,bkhmd->bhgmqk', q, k_past).astype(jnp.float32) * scale
    s_n = jnp.einsum('bqhgmd,bkhmd->bhgmqk', q, k).astype(jnp.float32) * scale
    s_n = jnp.where(jnp.tril(jnp.ones((t, t), bool)), s_n, -jnp.inf)
    a = _diff_weights(jax.nn.softmax(jnp.concatenate([s_p, s_n], axis=-1), axis=-1), lam)
    return (jnp.einsum('bhgqk,bkhv->bqhgv', a[..., :n_past], v_past.astype(jnp.float32))
            + jnp.einsum('bhgqk,bkhv->bqhgv', a[..., n_past:], v.astype(jnp.float32)))


def _cross(qx, mk, mv):
    b, t = qx.shape[:2]
    sc = jnp.einsum('bthd,bmhd->bhtm', qx, mk).astype(jnp.float32) * (X_DIM ** -0.5)
    p = jax.nn.softmax(sc, axis=-1)
    return jnp.einsum('bhtm,bmhd->bthd', p, mv.astype(jnp.float32)).reshape(b, t, X_WIDTH)


def _moe(h, w_router, b_router, w_up, b_up, w_down, b_down):
    n = h.shape[0]
    logits = jnp.dot(h.astype(jnp.float32), w_router.astype(jnp.float32)) + b_router.astype(jnp.float32)
    top_v, top_e = lax.top_k(logits, TOP_K)
    gate = jax.nn.softmax(top_v, axis=-1)
    nk = n * TOP_K
    flat_e = top_e.reshape(nk)
    flat_tok = jnp.arange(nk, dtype=jnp.int32) // TOP_K
    order = jnp.argsort(flat_e)
    se, stok, sgate = flat_e[order], flat_tok[order], gate.reshape(nk)[order]
    counts = jnp.bincount(flat_e, length=N_EXPERTS).astype(jnp.int32)
    padded = (counts + MOE_BLOCK - 1) // MOE_BLOCK * MOE_BLOCK
    g_start = jnp.cumsum(counts) - counts
    p_end = jnp.cumsum(padded)
    p_start = p_end - padded
    dest = p_start[se] + jnp.arange(nk, dtype=jnp.int32) - g_start[se]
    n_blocks = -(-(nk + N_EXPERTS * (MOE_BLOCK - 1)) // MOE_BLOCK)
    buf_tok = jnp.full((n_blocks * MOE_BLOCK,), n, jnp.int32).at[dest].set(stok)
    block_e = jnp.minimum(jnp.searchsorted(p_end, jnp.arange(n_blocks, dtype=jnp.int32) * MOE_BLOCK, side='right'), N_EXPERTS - 1)
    h_pad = jnp.concatenate([h, jnp.zeros((1, D_MODEL), h.dtype)], axis=0)
    def run_block(args):
        tok, e = args
        up = h_pad[tok] @ w_up[e] + b_up[e]
        glu = jnp.minimum(up[:, :D_FF], SWIGLU_LIMIT)
        lin = jnp.clip(up[:, D_FF:], -SWIGLU_LIMIT, SWIGLU_LIMIT)
        act = glu * jax.nn.sigmoid(SWIGLU_ALPHA * glu) * (lin + 1.0)
        return act @ w_down[e] + b_down[e]
    out_buf = lax.map(run_block, (buf_tok.reshape(n_blocks, MOE_BLOCK), block_e)).reshape(n_blocks * MOE_BLOCK, D_MODEL)
    y = jnp.zeros((n, D_MODEL), jnp.float32).at[stok].add(sgate[:, None] * out_buf[dest].astype(jnp.float32))
    return y.astype(h.dtype)


def _layer(x, hist, mem_k, mem_v, kv_past, lp, lam_init):
    b, t = x.shape[:2]
    h = _rms(x, lp['g_mix'])
    u, q, k, v, qx, gl = _split(h @ lp['w_in'])
    q = q.reshape(b, t, N_KV_HEADS, GQA_GROUP, 2, HEAD_DIM)
    k = k.reshape(b, t, N_KV_HEADS, 2, HEAD_DIM)
    v = v.reshape(b, t, N_KV_HEADS, V_DIM)
    qx = qx.reshape(b, t, X_HEADS, X_DIM)
    a_pool = _pool_branch(u, hist, lp['w_pool_group'], lp['pool_scale'])
    lam = (jnp.exp(jnp.sum(lp['lambda_q1'].astype(jnp.float32) * lp['lambda_k1'].astype(jnp.float32)))
           - jnp.exp(jnp.sum(lp['lambda_q2'].astype(jnp.float32) * lp['lambda_k2'].astype(jnp.float32))) + lam_init)
    if kv_past is None:
        o = _diff_prompt(q, k, v, lam)
    else:
        o = _diff_sample(q, k, v, kv_past[0], kv_past[1], lam)
    a_diff = (_rms(o, lp['g_subln']) * (1.0 - lam_init)).reshape(b, t, DIFF_WIDTH)
    a_cross = _cross(qx, mem_k, mem_v)
    gates = jax.nn.sigmoid(gl.astype(jnp.float32)).reshape(b, t, N_BRANCH, D_MODEL)
    merged = (gates[:, :, 0] * (a_pool @ lp['w_pool_proj'].astype(jnp.float32))
              + gates[:, :, 1] * (a_diff @ lp['w_diff_proj'].astype(jnp.float32))
              + gates[:, :, 2] * (a_cross @ lp['w_cross_proj'].astype(jnp.float32)))
    x = x + merged.astype(x.dtype) @ lp['w_o']
    h2 = _rms(x, lp['g_ffn']).reshape(b * t, D_MODEL)
    x = x + _moe(h2, lp['w_router'], lp['b_router'], lp['w_up'], lp['b_up'], lp['w_down'], lp['b_down']).reshape(b, t, D_MODEL)
    new_hist = jnp.concatenate([hist, u.astype(hist.dtype)], axis=1)[:, -POOL_STATE:]
    return x, k, v, new_hist


def setup_inputs(seed: int = 0) -> dict:
    key = jax.random.key(seed)
    ks = jax.random.split(key, 40)
    f32 = jnp.float32
    def nrm(k, shape, scale=1.0):
        return jax.random.normal(k, shape, f32) * scale
    n_pages = PAST_LEN // PAGE_SIZE
    n_used = DEC_BATCH * n_pages
    n_phys = (5 * n_used + 3) // 4
    page_table = jax.random.permutation(ks[9], n_phys)[:n_used].reshape(DEC_BATCH, n_pages).astype(jnp.int32)
    L = DEPTH
    return {
        'x_prompt': nrm(ks[0], (BATCH, SEQ, D_MODEL)),
        'x_sample': nrm(ks[1], (DEC_BATCH, DEC_SEQ, D_MODEL)),
        'mem_prompt': nrm(ks[2], (BATCH, N_MEM, D_MODEL)),
        'cache_k': nrm(ks[3], (L, n_phys, PAGE_SIZE, N_KV_HEADS, 2, HEAD_DIM)),
        'cache_v': nrm(ks[4], (L, n_phys, PAGE_SIZE, N_KV_HEADS, V_DIM)),
        'cache_mem_k': nrm(ks[5], (L, DEC_BATCH, N_MEM, X_HEADS, X_DIM)),
        'cache_mem_v': nrm(ks[6], (L, DEC_BATCH, N_MEM, X_HEADS, X_DIM)),
        'state_pool': nrm(ks[7], (L, DEC_BATCH, POOL_STATE, POOL_WIDTH)),
        'page_table': page_table,
        'g_mix': 1.0 + nrm(ks[10], (L, D_MODEL), 0.05),
        'w_in': nrm(ks[11], (L, D_MODEL, IN_WIDTH), D_MODEL ** -0.5),
        'w_pool_group': nrm(ks[12], (L, POOL_GROUPS, POOL_GROUP_DIM, POOL_GROUP_DIM), POOL_GROUP_DIM ** -0.5),
        'pool_scale': 1.0 + nrm(ks[13], (L, POOL_WIDTH), 0.1),
        'lambda_q1': nrm(ks[14], (L, HEAD_DIM), 0.1),
        'lambda_k1': nrm(ks[15], (L, HEAD_DIM), 0.1),
        'lambda_q2': nrm(ks[16], (L, HEAD_DIM), 0.1),
        'lambda_k2': nrm(ks[17], (L, HEAD_DIM), 0.1),
        'g_subln': 1.0 + nrm(ks[18], (L, V_DIM), 0.05),
        'g_mem': 1.0 + nrm(ks[19], (L, D_MODEL), 0.05),
        'w_mem_kv': nrm(ks[20], (L, D_MODEL, 2 * X_WIDTH), D_MODEL ** -0.5),
        'w_pool_proj': nrm(ks[21], (L, POOL_WIDTH, D_MODEL), POOL_WIDTH ** -0.5),
        'w_diff_proj': nrm(ks[22], (L, DIFF_WIDTH, D_MODEL), DIFF_WIDTH ** -0.5),
        'w_cross_proj': nrm(ks[23], (L, X_WIDTH, D_MODEL), X_WIDTH ** -0.5),
        'w_o': nrm(ks[24], (L, D_MODEL, D_MODEL), D_MODEL ** -0.5),
        'g_ffn': 1.0 + nrm(ks[25], (L, D_MODEL), 0.05),
        'w_router': nrm(ks[26], (L, D_MODEL, N_EXPERTS), D_MODEL ** -0.5),
        'b_router': nrm(ks[27], (L, N_EXPERTS), 0.01),
        'w_up': nrm(ks[28], (L, N_EXPERTS, D_MODEL, 2 * D_FF), D_MODEL ** -0.5),
        'b_up': nrm(ks[29], (L, N_EXPERTS, 2 * D_FF), 0.01),
        'w_down': nrm(ks[30], (L, N_EXPERTS, D_FF, D_MODEL), D_FF ** -0.5),
        'b_down': nrm(ks[31], (L, N_EXPERTS, D_MODEL), 0.01),
        'g_final': 1.0 + nrm(ks[32], (D_MODEL,), 0.05),
    }


def reference(x_prompt, x_sample, mem_prompt, cache_k, cache_v, cache_mem_k, cache_mem_v, state_pool, page_table,
              g_mix, w_in, w_pool_group, pool_scale, lambda_q1, lambda_k1, lambda_q2, lambda_k2, g_subln, g_mem,
              w_mem_kv, w_pool_proj, w_diff_proj, w_cross_proj, w_o, g_ffn, w_router, b_router, w_up, b_up,
              w_down, b_down, g_final):
    b = x_prompt.shape[0]
    db = x_sample.shape[0]
    xp, xs = x_prompt, x_sample
    kp_l, vp_l, mk_l, mv_l, hp_l, ks_l, vs_l, hs_l = [], [], [], [], [], [], [], []
    for l in range(DEPTH):
        lp = {'g_mix': g_mix[l], 'w_in': w_in[l], 'w_pool_group': w_pool_group[l], 'pool_scale': pool_scale[l],
              'lambda_q1': lambda_q1[l], 'lambda_k1': lambda_k1[l], 'lambda_q2': lambda_q2[l], 'lambda_k2': lambda_k2[l],
              'g_subln': g_subln[l], 'w_pool_proj': w_pool_proj[l], 'w_diff_proj': w_diff_proj[l],
              'w_cross_proj': w_cross_proj[l], 'w_o': w_o[l], 'g_ffn': g_ffn[l], 'w_router': w_router[l],
              'b_router': b_router[l], 'w_up': w_up[l], 'b_up': b_up[l], 'w_down': w_down[l], 'b_down': b_down[l]}
        lam_init = 0.8 - 0.6 * math.exp(-0.3 * l)
        mkv = _rms(mem_prompt, g_mem[l]) @ w_mem_kv[l]
        mk = mkv[..., :X_WIDTH].reshape(b, N_MEM, X_HEADS, X_DIM)
        mv = mkv[..., X_WIDTH:].reshape(b, N_MEM, X_HEADS, X_DIM)
        hist0 = jnp.zeros((b, 0, POOL_WIDTH), state_pool.dtype)
        xp, kp, vp, hp = _layer(xp, hist0, mk, mv, None, lp, lam_init)
        k_past = cache_k[l, page_table].reshape(db, -1, N_KV_HEADS, 2, HEAD_DIM)
        v_past = cache_v[l, page_table].reshape(db, -1, N_KV_HEADS, V_DIM)
        xs, ksm, vsm, hsm = _layer(xs, state_pool[l], cache_mem_k[l], cache_mem_v[l], (k_past, v_past), lp, lam_init)
        kp_l.append(kp); vp_l.append(vp); mk_l.append(mk); mv_l.append(mv); hp_l.append(hp)
        ks_l.append(ksm); vs_l.append(vsm); hs_l.append(hsm)
    y_prompt = _rms(xp, g_final)
    y_sample = _rms(xs, g_final)
    return (y_prompt, y_sample, jnp.stack(kp_l), jnp.stack(vp_l), jnp.stack(mk_l), jnp.stack(mv_l), jnp.stack(hp_l),
            jnp.stack(ks_l), jnp.stack(vs_l), jnp.stack(hs_l))
```

```python
import functools
import math

import numpy as np
import jax
import jax.numpy as jnp
from jax import lax
from jax.experimental import pallas as pl
from jax.experimental.pallas import tpu as pltpu

F32 = jnp.float32
BF16 = jnp.bfloat16

D_MODEL = 1024
N_HEADS = 8
HEAD_DIM = 64
N_KV_HEADS = 4
GQA_GROUP = N_HEADS // N_KV_HEADS
V_DIM = 2 * HEAD_DIM
POOL_WINDOWS = (2, 4, 8, 16)
POOL_GROUPS = 4
POOL_GROUP_DIM = D_MODEL // POOL_GROUPS
POOL_STATE = 15
POOL_HALO = 16
X_HEADS = 4
X_DIM = D_MODEL // X_HEADS
N_BRANCH = 3
N_EXPERTS = 32
TOP_K = 4
D_FF = D_MODEL
SWIGLU_LIMIT = 7.0
SWIGLU_ALPHA = 1.702
RMS_EPS = 1e-6
Q_WIDTH = N_HEADS * 2 * HEAD_DIM
K_WIDTH = N_KV_HEADS * 2 * HEAD_DIM
V_WIDTH = N_KV_HEADS * V_DIM
KV_BLOCK = 2 * HEAD_DIM
MASK_VALUE = -1e30

LANES = 128
EXPERT_LANES = LANES
MIB = 2 ** 20

ROW_TILE = 256
ATTN_TQ = 512
ATTN_TK = 512
PAGES_PER_STEP = 8
CROSS_TQ = 512
MOE_ROWS = 256
POOL_SEQ_BLOCK = 8


def _cparams(semantics, vmem_mib):
    return pltpu.CompilerParams(dimension_semantics=semantics, vmem_limit_bytes=vmem_mib * MIB)


def _const_spec(shape):
    zeros = (0,) * len(shape)
    return pl.BlockSpec(shape, lambda *_: zeros)


def _rms(xf, g):
    return xf * lax.rsqrt(jnp.mean(xf * xf, axis=-1, keepdims=True) + RMS_EPS) * g


def _rms_proj_kernel(x_ref, g_ref, w_ref, *out_refs, splits):
    hb = _rms(x_ref[...].astype(F32), g_ref[...]).astype(BF16)
    oi = 0
    for c0, width, scale, dtypes in splits:
        z = jnp.dot(hb, w_ref[:, c0:c0 + width], preferred_element_type=F32)
        for dt in dtypes:
            out_refs[oi][...] = (z * scale).astype(dt) if dt != F32 or scale != 1.0 else z
            oi += 1


def _rms_proj(x2d, g, w_bf, splits):
    n, d = x2d.shape
    tm = ROW_TILE
    assert n % tm == 0
    out_shape, out_specs = [], []
    for _, width, _, dtypes in splits:
        for dt in dtypes:
            out_shape.append(jax.ShapeDtypeStruct((n, width), dt))
            out_specs.append(pl.BlockSpec((tm, width), lambda i: (i, 0)))
    return pl.pallas_call(
        functools.partial(_rms_proj_kernel, splits=splits),
        grid=(n // tm,),
        in_specs=[pl.BlockSpec((tm, d), lambda i: (i, 0)), _const_spec((1, d)), _const_spec(w_bf.shape)],
        out_specs=out_specs,
        out_shape=out_shape,
        compiler_params=_cparams(("parallel",), 56),
        name="rms_proj",
    )(x2d, g.reshape(1, d), w_bf)


def _pool_prompt_kernel(u_ref, halo_ref, o_ref, ue, *, tm):
    i = pl.program_id(1)
    ue[0:POOL_HALO, :] = jnp.where(i == 0, 0.0, halo_ref[...])
    ue[POOL_HALO:, :] = u_ref[...]
    pos = i * tm + lax.broadcasted_iota(jnp.int32, (tm, 1), 0)
    for g, w in enumerate(POOL_WINDOWS):
        c = slice(g * POOL_GROUP_DIM, (g + 1) * POOL_GROUP_DIM)
        tok = ue[POOL_HALO:POOL_HALO + tm, c]
        acc = tok
        for j in range(1, w):
            acc = acc + ue[POOL_HALO - j:POOL_HALO - j + tm, c]
        cnt = jnp.minimum(pos + 1, w).astype(F32)
        o_ref[:, c] = (acc / cnt - tok).astype(o_ref.dtype)


def _pool_prompt(u3):
    b, s, d = u3.shape
    tm = ROW_TILE
    halo_blocks = tm // POOL_HALO
    return pl.pallas_call(
        functools.partial(_pool_prompt_kernel, tm=tm),
        grid=(b, s // tm),
        in_specs=[pl.BlockSpec((None, tm, d), lambda bi, i: (bi, i, 0)),
                  pl.BlockSpec((None, POOL_HALO, d), lambda bi, i: (bi, jnp.maximum(i * halo_blocks - 1, 0), 0))],
        out_specs=pl.BlockSpec((None, tm, d), lambda bi, i: (bi, i, 0)),
        out_shape=jax.ShapeDtypeStruct((b, s, d), BF16),
        scratch_shapes=[pltpu.VMEM((tm + POOL_HALO, d), F32)],
        compiler_params=_cparams(("parallel", "parallel"), 32),
        name="pool_prompt",
    )(u3, u3)


def _pool_sample_kernel(ue_ref, o_ref, *, n_new):
    lo = ue_ref.shape[1] - n_new
    for s in range(ue_ref.shape[0]):
        for g, w in enumerate(POOL_WINDOWS):
            c = slice(g * POOL_GROUP_DIM, (g + 1) * POOL_GROUP_DIM)
            tok = ue_ref[s, lo:lo + n_new, c]
            acc = tok
            for j in range(1, w):
                acc = acc + ue_ref[s, lo - j:lo - j + n_new, c]
            o_ref[s, :, c] = acc / float(w) - tok


def _pool_sample(hist, u3):
    db, t, d = u3.shape
    assert hist.shape[1] >= POOL_WINDOWS[-1] - 1
    pad = (-(hist.shape[1] + t)) % 8
    ue = jnp.concatenate([jnp.zeros((db, pad, d), F32), hist.astype(F32), u3], axis=1)
    rows = ue.shape[1]
    sb = POOL_SEQ_BLOCK
    return pl.pallas_call(
        functools.partial(_pool_sample_kernel, n_new=t),
        grid=(db // sb,),
        in_specs=[pl.BlockSpec((sb, rows, d), lambda i: (i, 0, 0))],
        out_specs=pl.BlockSpec((sb, t, d), lambda i: (i, 0, 0)),
        out_shape=jax.ShapeDtypeStruct((db, t, d), F32),
        compiler_params=_cparams(("parallel",), 32),
        name="pool_sample",
    )(ue)


def _lambda_value(lam_ref, lam_init):
    a = jnp.sum(lam_ref[0:1, :] * lam_ref[1:2, :], axis=-1, keepdims=True)
    b = jnp.sum(lam_ref[2:3, :] * lam_ref[3:4, :], axis=-1, keepdims=True)
    return jnp.exp(a) - jnp.exp(b) + lam_init


def _subln(o, gsub, lam_init):
    return _rms(o, gsub) * (1.0 - lam_init)


def _online_softmax_step(s, m_prev, l_prev):
    m_cur = jnp.max(s, axis=-1, keepdims=True)
    m_new = jnp.maximum(m_prev, m_cur)
    alpha = jnp.exp(m_prev - m_new)
    p = jnp.exp(s - m_new[:, 0:1])
    l_new = alpha * l_prev + jnp.sum(p, axis=-1, keepdims=True)
    return p, m_new, l_new, alpha


def _attn_prompt_kernel(qi_ref, ki_ref, last_ref, q_ref, k_ref, v_ref, lam_ref, gsub_ref, o_ref,
                        qpad, acc, m_scr, l_scr, *, tq, tk, lam_init):
    step = pl.program_id(1)
    qi = qi_ref[step]
    ki = ki_ref[step]

    @pl.when(ki == 0)
    def _init():
        lane = lax.broadcasted_iota(jnp.int32, (tq, KV_BLOCK), 1)
        for h in range(N_KV_HEADS):
            for g in range(GQA_GROUP):
                c0 = (h * GQA_GROUP + g) * KV_BLOCK
                blk = q_ref[:, c0:c0 + KV_BLOCK]
                rows = slice(g * tq, (g + 1) * tq)
                qpad[2 * h, rows, :] = jnp.where(lane < HEAD_DIM, blk, jnp.zeros_like(blk))
                qpad[2 * h + 1, rows, :] = jnp.where(lane >= HEAD_DIM, blk, jnp.zeros_like(blk))
        acc[...] = jnp.zeros_like(acc)
        m_scr[...] = jnp.full_like(m_scr, MASK_VALUE)
        l_scr[...] = jnp.zeros_like(l_scr)

    def body(masked):
        if masked:
            qpos = qi * tq + lax.broadcasted_iota(jnp.int32, (tq, tk), 0)
            kpos = ki * tk + lax.broadcasted_iota(jnp.int32, (tq, tk), 1)
            vis1 = kpos <= qpos
            vis = jnp.concatenate([vis1] * GQA_GROUP, axis=0)
        for h in range(N_KV_HEADS):
            kb = k_ref[:, h * KV_BLOCK:(h + 1) * KV_BLOCK]
            vb = v_ref[:, h * V_DIM:(h + 1) * V_DIM]
            for m in range(2):
                idx = 2 * h + m
                s = lax.dot_general(qpad[idx], kb, (((1,), (1,)), ((), ())), preferred_element_type=F32)
                if masked:
                    s = jnp.where(vis, s, MASK_VALUE)
                p, m_new, l_new, alpha = _online_softmax_step(s, m_scr[idx], l_scr[idx])
                m_scr[idx] = m_new
                l_scr[idx] = l_new
                acc[idx] = alpha * acc[idx] + jnp.dot(p.astype(BF16), vb, preferred_element_type=F32)

    needs_mask = (ki + 1) * tk - 1 > qi * tq

    @pl.when(needs_mask)
    def _masked():
        body(True)

    @pl.when(jnp.logical_not(needs_mask))
    def _unmasked():
        body(False)

    @pl.when(last_ref[step] == 1)
    def _finish():
        lam = _lambda_value(lam_ref, lam_init)
        gsub = gsub_ref[...]
        for h in range(N_KV_HEADS):
            for g in range(GQA_GROUP):
                rows = slice(g * tq, (g + 1) * tq)
                o1 = acc[2 * h, rows, :] / l_scr[2 * h, rows, :]
                o2 = acc[2 * h + 1, rows, :] / l_scr[2 * h + 1, rows, :]
                c0 = (h * GQA_GROUP + g) * V_DIM
                o_ref[:, c0:c0 + V_DIM] = _subln(o1 - lam * o2, gsub, lam_init).astype(o_ref.dtype)


def _attn_prompt(q, k, v, lam_vecs, gsub, lam_init):
    b, s, _ = q.shape
    tq, tk = ATTN_TQ, ATTN_TK
    assert s % tq == 0 and s % tk == 0
    qi_l, ki_l, last_l = [], [], []
    for qi in range(s // tq):
        n_k = min(((qi + 1) * tq - 1) // tk + 1, s // tk)
        for ki in range(n_k):
            qi_l.append(qi)
            ki_l.append(ki)
            last_l.append(1 if ki == n_k - 1 else 0)
    tabs = [jnp.asarray(np.array(t, np.int32)) for t in (qi_l, ki_l, last_l)]
    rows = GQA_GROUP * tq
    grid_spec = pltpu.PrefetchScalarGridSpec(
        num_scalar_prefetch=3,
        grid=(b, len(qi_l)),
        in_specs=[pl.BlockSpec((None, tq, Q_WIDTH), lambda bi, st, qt, kt, lt: (bi, qt[st], 0)),
                  pl.BlockSpec((None, tk, K_WIDTH), lambda bi, st, qt, kt, lt: (bi, kt[st], 0)),
                  pl.BlockSpec((None, tk, V_WIDTH), lambda bi, st, qt, kt, lt: (bi, kt[st], 0)),
                  _const_spec((4, HEAD_DIM)), _const_spec((1, V_DIM))],
        out_specs=pl.BlockSpec((None, tq, N_HEADS * V_DIM), lambda bi, st, qt, kt, lt: (bi, qt[st], 0)),
        scratch_shapes=[pltpu.VMEM((2 * N_KV_HEADS, rows, KV_BLOCK), BF16),
                        pltpu.VMEM((2 * N_KV_HEADS, rows, V_DIM), F32),
                        pltpu.VMEM((2 * N_KV_HEADS, rows, LANES), F32),
                        pltpu.VMEM((2 * N_KV_HEADS, rows, LANES), F32)])
    return pl.pallas_call(
        functools.partial(_attn_prompt_kernel, tq=tq, tk=tk, lam_init=lam_init),
        grid_spec=grid_spec,
        out_shape=jax.ShapeDtypeStruct((b, s, N_HEADS * V_DIM), BF16),
        compiler_params=_cparams(("parallel", "arbitrary"), 56),
        name="attn_prompt",
    )(*tabs, q, k, v, lam_vecs, gsub.reshape(1, V_DIM))


SAMPLE_ROWS = N_KV_HEADS * 2 * GQA_GROUP


def _attn_sample_kernel(pt_ref, wq_ref, kn_ref, vn_ref, lam_ref, gsub_ref, *rest, n_pages, t_new, lam_init):
    k_refs = rest[:n_pages]
    v_refs = rest[n_pages:2 * n_pages]
    o_ref, acc, m_scr, l_scr = rest[2 * n_pages:]
    step = pl.program_id(1)
    rows = SAMPLE_ROWS * t_new

    @pl.when(step == 0)
    def _init():
        acc[...] = jnp.zeros_like(acc)
        m_scr[...] = jnp.full_like(m_scr, MASK_VALUE)
        l_scr[...] = jnp.zeros_like(l_scr)

    wq = wq_ref[...].astype(BF16)

    def update(s, v_list):
        p, m_new, l_new, alpha = _online_softmax_step(s, m_scr[...], l_scr[...])
        m_scr[...] = m_new
        l_scr[...] = l_new
        pb = p.astype(BF16)
        pv = None
        for j, vj in enumerate(v_list):
            c = vj.shape[0]
            term = jnp.dot(pb[:, j * c:(j + 1) * c], vj, preferred_element_type=F32)
            pv = term if pv is None else pv + term
        acc[...] = alpha[:, 0:1] * acc[...] + pv

    s_list = [lax.dot_general(wq, kr[...].astype(BF16), (((1,), (1,)), ((), ())), preferred_element_type=F32)
              for kr in k_refs]
    update(jnp.concatenate(s_list, axis=1), [vr[...].astype(BF16) for vr in v_refs])

    @pl.when(step == pl.num_programs(1) - 1)
    def _finish():
        pad = jnp.zeros((LANES - t_new, kn_ref.shape[1]), F32)
        kn = jnp.concatenate([kn_ref[...].astype(F32), pad], axis=0).astype(BF16)
        vn = jnp.concatenate([vn_ref[...].astype(F32), pad], axis=0).astype(BF16)
        s_new = lax.dot_general(wq, kn, (((1,), (1,)), ((), ())), preferred_element_type=F32)
        t_of_row = lax.broadcasted_iota(jnp.int32, (rows, LANES), 0) % t_new
        key = lax.broadcasted_iota(jnp.int32, (rows, LANES), 1)
        update(jnp.where(key <= t_of_row, s_new, MASK_VALUE), [vn])

        lam = _lambda_value(lam_ref, lam_init)
        gsub = gsub_ref[...]
        for h in range(N_KV_HEADS):
            for g in range(GQA_GROUP):
                r1 = (h * 2 * GQA_GROUP + g) * t_new
                r2 = (h * 2 * GQA_GROUP + GQA_GROUP + g) * t_new
                cols = slice(h * V_DIM, (h + 1) * V_DIM)
                o1 = acc[r1:r1 + t_new, cols] / l_scr[r1:r1 + t_new, :]
                o2 = acc[r2:r2 + t_new, cols] / l_scr[r2:r2 + t_new, :]
                c0 = (h * GQA_GROUP + g) * V_DIM
                o_ref[:, c0:c0 + V_DIM] = _subln(o1 - lam * o2, gsub, lam_init).astype(o_ref.dtype)


def _sample_query_rows(q3):
    db, t, _ = q3.shape
    q6 = q3.reshape(db, t, N_KV_HEADS, GQA_GROUP, 2, HEAD_DIM).transpose(0, 2, 4, 3, 1, 5)
    eye = jnp.eye(N_KV_HEADS * 2, dtype=q3.dtype).reshape(N_KV_HEADS, 2, N_KV_HEADS, 2)
    wq = jnp.einsum("bhmgtd,hmHM->bhmgtHMd", q6, eye)
    return wq.reshape(db, SAMPLE_ROWS * t, K_WIDTH).astype(BF16)


def _attn_sample(q3, k_new, v_new, cache_k, cache_v, page_ids, lam_vecs, gsub, lam_init):
    db, t, _ = q3.shape
    page = cache_k.shape[1]
    n_pages_total = page_ids.shape[1]
    g = PAGES_PER_STEP
    assert n_pages_total % g == 0 and page == LANES and SAMPLE_ROWS * t == LANES
    wq = _sample_query_rows(q3)
    rows = SAMPLE_ROWS * t

    def page_spec(j):
        return pl.BlockSpec((None, page, K_WIDTH), lambda bi, st, pt: (pt[bi, st * g + j], 0, 0))

    grid_spec = pltpu.PrefetchScalarGridSpec(
        num_scalar_prefetch=1,
        grid=(db, n_pages_total // g),
        in_specs=[pl.BlockSpec((None, rows, K_WIDTH), lambda bi, st, pt: (bi, 0, 0)),
                  pl.BlockSpec((None, t, K_WIDTH), lambda bi, st, pt: (bi, 0, 0)),
                  pl.BlockSpec((None, t, V_WIDTH), lambda bi, st, pt: (bi, 0, 0)),
                  _const_spec((4, HEAD_DIM)), _const_spec((1, V_DIM))]
                 + [page_spec(j) for j in range(g)] + [page_spec(j) for j in range(g)],
        out_specs=pl.BlockSpec((None, t, N_HEADS * V_DIM), lambda bi, st, pt: (bi, 0, 0)),
        scratch_shapes=[pltpu.VMEM((rows, V_WIDTH), F32),
                        pltpu.VMEM((rows, LANES), F32),
                        pltpu.VMEM((rows, LANES), F32)])
    return pl.pallas_call(
        functools.partial(_attn_sample_kernel, n_pages=g, t_new=t, lam_init=lam_init),
        grid_spec=grid_spec,
        out_shape=jax.ShapeDtypeStruct((db, t, N_HEADS * V_DIM), F32),
        compiler_params=_cparams(("parallel", "arbitrary"), 40),
        name="attn_sample",
    )(page_ids, wq, k_new, v_new, lam_vecs, gsub.reshape(1, V_DIM), *([cache_k] * g), *([cache_v] * g))


def _cross_kernel(q_ref, mk_ref, mv_ref, o_ref):
    for h in range(X_HEADS):
        c = slice(h * X_DIM, (h + 1) * X_DIM)
        q = q_ref[:, c].astype(BF16)
        s = lax.dot_general(q, mk_ref[:, c].astype(BF16), (((1,), (1,)), ((), ())), preferred_element_type=F32)
        p = jnp.exp(s - jnp.max(s, axis=-1, keepdims=True))
        l = jnp.sum(p, axis=-1, keepdims=True)
        o = jnp.dot(p.astype(BF16), mv_ref[:, c].astype(BF16), preferred_element_type=F32)
        o_ref[:, c] = (o / l).astype(o_ref.dtype)


def _cross(q3, mk, mv, mem_offset, out_dtype):
    b, t, d = q3.shape
    tq = min(CROSS_TQ, t)
    n_mem = mk.shape[1]
    return pl.pallas_call(
        _cross_kernel,
        grid=(b, t // tq),
        in_specs=[pl.BlockSpec((None, tq, d), lambda bi, i: (bi, i, 0)),
                  pl.BlockSpec((None, n_mem, d), lambda bi, i: (bi + mem_offset, 0, 0)),
                  pl.BlockSpec((None, n_mem, d), lambda bi, i: (bi + mem_offset, 0, 0))],
        out_specs=pl.BlockSpec((None, tq, d), lambda bi, i: (bi, i, 0)),
        out_shape=jax.ShapeDtypeStruct((b, t, d), out_dtype),
        compiler_params=_cparams(("parallel", "parallel"), 32),
        name="cross_attn",
    )(q3, mk, mv)


def _merge_kernel(pooled_ref, adiff_ref, across_ref, gl_ref, x_ref, wg_ref, ps_ref, wpp_ref, wdp_ref, wcp_ref,
                  wo_ref, gffn_ref, wr_ref, br_ref, x1_ref, h2_ref, lg_ref):
    pooled = pooled_ref[...].astype(BF16)
    ys = [jnp.dot(pooled[:, g * POOL_GROUP_DIM:(g + 1) * POOL_GROUP_DIM], wg_ref[g], preferred_element_type=F32)
          for g in range(POOL_GROUPS)]
    a_pool = jnp.concatenate(ys, axis=1) * ps_ref[...]
    p_pool = jnp.dot(a_pool.astype(BF16), wpp_ref[...], preferred_element_type=F32)
    p_diff = jnp.dot(adiff_ref[...].astype(BF16), wdp_ref[...], preferred_element_type=F32)
    p_cross = jnp.dot(across_ref[...].astype(BF16), wcp_ref[...], preferred_element_type=F32)
    merged = (jax.nn.sigmoid(gl_ref[:, 0:D_MODEL]) * p_pool
              + jax.nn.sigmoid(gl_ref[:, D_MODEL:2 * D_MODEL]) * p_diff
              + jax.nn.sigmoid(gl_ref[:, 2 * D_MODEL:3 * D_MODEL]) * p_cross)
    x1 = x_ref[...] + jnp.dot(merged.astype(BF16), wo_ref[...], preferred_element_type=F32)
    x1_ref[...] = x1
    h2 = _rms(x1, gffn_ref[...]).astype(BF16)
    h2_ref[...] = h2
    lg_ref[...] = jnp.dot(h2, wr_ref[...], preferred_element_type=F32) + br_ref[...]


def _merge(pooled, adiff, across, gl, x2d, wts):
    n, d = x2d.shape
    tm = ROW_TILE
    row = lambda w: pl.BlockSpec((tm, w), lambda i: (i, 0))
    consts = [wts["wg"], wts["pool_scale"], wts["wpp"], wts["wdp"], wts["wcp"], wts["wo"], wts["g_ffn"],
              wts["wr"], wts["br"]]
    return pl.pallas_call(
        _merge_kernel,
        grid=(n // tm,),
        in_specs=[row(d), row(d), row(d), row(N_BRANCH * d), row(d)] + [_const_spec(c.shape) for c in consts],
        out_specs=[row(d), row(d), row(EXPERT_LANES)],
        out_shape=[jax.ShapeDtypeStruct((n, d), F32), jax.ShapeDtypeStruct((n, d), BF16),
                   jax.ShapeDtypeStruct((n, EXPERT_LANES), F32)],
        compiler_params=_cparams(("parallel",), 48),
        name="merge",
    )(pooled, adiff, across, gl, x2d, *consts)


def _expert_kernel(be_ref, nv_ref, xs_ref, wu_ref, bu_ref, wd_ref, bd_ref, o_ref):
    i = pl.program_id(0)

    @pl.when(i < nv_ref[0])
    def _run():
        up = jnp.dot(xs_ref[...], wu_ref[...], preferred_element_type=F32) + bu_ref[...]
        glu = jnp.minimum(up[:, :D_FF], SWIGLU_LIMIT)
        lin = jnp.clip(up[:, D_FF:], -SWIGLU_LIMIT, SWIGLU_LIMIT)
        act = glu * jax.nn.sigmoid(SWIGLU_ALPHA * glu) * (lin + 1.0)
        o_ref[...] = jnp.dot(act.astype(BF16), wd_ref[...], preferred_element_type=F32) + bd_ref[...]

    @pl.when(i >= nv_ref[0])
    def _skip():
        o_ref[...] = jnp.zeros_like(o_ref)


def _experts(xs, block_e, n_valid, wu, bu, wd, bd):
    rows, d = xs.shape
    tm = MOE_ROWS
    grid_spec = pltpu.PrefetchScalarGridSpec(
        num_scalar_prefetch=2,
        grid=(rows // tm,),
        in_specs=[pl.BlockSpec((tm, d), lambda i, be, nv: (i, 0)),
                  pl.BlockSpec((None, d, 2 * D_FF), lambda i, be, nv: (be[i], 0, 0)),
                  pl.BlockSpec((None, 1, 2 * D_FF), lambda i, be, nv: (be[i], 0, 0)),
                  pl.BlockSpec((None, D_FF, d), lambda i, be, nv: (be[i], 0, 0)),
                  pl.BlockSpec((None, 1, d), lambda i, be, nv: (be[i], 0, 0))],
        out_specs=pl.BlockSpec((tm, d), lambda i, be, nv: (i, 0)))
    return pl.pallas_call(
        _expert_kernel,
        grid_spec=grid_spec,
        out_shape=jax.ShapeDtypeStruct((rows, d), F32),
        compiler_params=_cparams(("arbitrary",), 48),
        name="experts",
    )(block_e, n_valid, xs, wu, bu, wd, bd)


def _final_kernel(x1_ref, rows_ref, gate_ref, gf_ref, y_ref):
    gate = gate_ref[...]
    moe = gate[:, 0:1] * rows_ref[0]
    for k in range(1, TOP_K):
        moe = moe + gate[:, k:k + 1] * rows_ref[k]
    y_ref[...] = _rms(x1_ref[...] + moe, gf_ref[...])


def _final(x1, rows, gate, g_final):
    n, d = x1.shape
    tm = ROW_TILE
    return pl.pallas_call(
        _final_kernel,
        grid=(n // tm,),
        in_specs=[pl.BlockSpec((tm, d), lambda i: (i, 0)),
                  pl.BlockSpec((TOP_K, tm, d), lambda i: (0, i, 0)),
                  pl.BlockSpec((tm, TOP_K), lambda i: (i, 0)),
                  _const_spec((1, d))],
        out_specs=pl.BlockSpec((tm, d), lambda i: (i, 0)),
        out_shape=jax.ShapeDtypeStruct((n, d), F32),
        compiler_params=_cparams(("parallel",), 32),
        name="final_norm",
    )(x1, rows, gate, g_final.reshape(1, d))


def _route(logits):
    n = logits.shape[0]
    top_v, top_e = lax.top_k(logits[:, :N_EXPERTS], TOP_K)
    gate = jax.nn.softmax(top_v, axis=-1)
    nk = n * TOP_K
    flat_e = top_e.reshape(nk)
    order = jnp.argsort(flat_e)
    se = flat_e[order]
    counts = jnp.bincount(flat_e, length=N_EXPERTS).astype(jnp.int32)
    padded = (counts + MOE_ROWS - 1) // MOE_ROWS * MOE_ROWS
    g_start = jnp.cumsum(counts) - counts
    p_end = jnp.cumsum(padded)
    p_start = p_end - padded
    dest_sorted = p_start[se] + jnp.arange(nk, dtype=jnp.int32) - g_start[se]
    n_blocks = -(-(nk + N_EXPERTS * (MOE_ROWS - 1)) // MOE_ROWS)
    buf_tok = jnp.full((n_blocks * MOE_ROWS,), n, jnp.int32).at[dest_sorted].set((order // TOP_K).astype(jnp.int32))
    dest = jnp.zeros((nk,), jnp.int32).at[order].set(dest_sorted).reshape(n, TOP_K)
    block_e = jnp.minimum(jnp.searchsorted(p_end, jnp.arange(n_blocks, dtype=jnp.int32) * MOE_ROWS, side="right"),
                          N_EXPERTS - 1).astype(jnp.int32)
    n_valid = (p_end[-1] // MOE_ROWS).astype(jnp.int32).reshape(1)
    return gate, dest, buf_tok, block_e, n_valid


def _in_splits(dt):
    off = np.cumsum([0, D_MODEL, Q_WIDTH, K_WIDTH, V_WIDTH, D_MODEL, N_BRANCH * D_MODEL])
    return ((int(off[0]), D_MODEL, 1.0, (F32,)),
            (int(off[1]), Q_WIDTH, HEAD_DIM ** -0.5, (dt,)),
            (int(off[2]), K_WIDTH, 1.0, (F32, BF16)),
            (int(off[3]), V_WIDTH, 1.0, (F32, BF16)),
            (int(off[4]), D_MODEL, X_DIM ** -0.5, (dt,)),
            (int(off[5]), N_BRANCH * D_MODEL, 1.0, (F32,)))


def kernel(x_prompt, x_sample, mem_prompt, cache_k, cache_v, cache_mem_k, cache_mem_v, state_pool, page_table, g_mix, w_in, w_pool_group, pool_scale, lambda_q1, lambda_k1, lambda_q2, lambda_k2, g_subln, g_mem, w_mem_kv, w_pool_proj, w_diff_proj, w_cross_proj, w_o, g_ffn, w_router, b_router, w_up, b_up, w_down, b_down, g_final):
    b, s, d = x_prompt.shape
    db, t, _ = x_sample.shape
    depth = w_in.shape[0]
    n_phys, page = cache_k.shape[1], cache_k.shape[2]
    n_mem = mem_prompt.shape[1]
    n_p, n_s = b * s, db * t

    cache_k3 = cache_k.reshape(depth * n_phys, page, K_WIDTH)
    cache_v3 = cache_v.reshape(depth * n_phys, page, V_WIDTH)
    cmk = cache_mem_k.reshape(depth * db, n_mem, d)
    cmv = cache_mem_v.reshape(depth * db, n_mem, d)

    xp = x_prompt.reshape(n_p, d)
    xs = x_sample.reshape(n_s, d)
    outs = {name: [] for name in ("kp", "vp", "mk", "mv", "hp", "ks", "vs", "hs")}

    for l in range(depth):
        lam_init = 0.8 - 0.6 * math.exp(-0.3 * l)
        lam_vecs = jnp.stack([lambda_q1[l], lambda_k1[l], lambda_q2[l], lambda_k2[l]]).astype(F32)
        w_in_bf = w_in[l].astype(BF16)
        wr = jnp.zeros((d, EXPERT_LANES), BF16).at[:, :N_EXPERTS].set(w_router[l].astype(BF16))
        br = jnp.full((1, EXPERT_LANES), MASK_VALUE, F32).at[0, :N_EXPERTS].set(b_router[l].astype(F32))
        wts = dict(wg=w_pool_group[l].astype(BF16), pool_scale=pool_scale[l].reshape(1, d).astype(F32),
                   wpp=w_pool_proj[l].astype(BF16), wdp=w_diff_proj[l].astype(BF16),
                   wcp=w_cross_proj[l].astype(BF16), wo=w_o[l].astype(BF16),
                   g_ffn=g_ffn[l].reshape(1, d).astype(F32), wr=wr, br=br)

        mk, mv = _rms_proj(mem_prompt.reshape(b * n_mem, d), g_mem[l], w_mem_kv[l].astype(BF16),
                           ((0, d, 1.0, (F32,)), (d, d, 1.0, (F32,))))
        mk3, mv3 = mk.reshape(b, n_mem, d), mv.reshape(b, n_mem, d)

        u_p, q_p, k_p, k_pb, v_p, v_pb, qx_p, gl_p = _rms_proj(xp, g_mix[l], w_in_bf, _in_splits(BF16))
        pooled_p = _pool_prompt(u_p.reshape(b, s, d)).reshape(n_p, d)
        adiff_p = _attn_prompt(q_p.reshape(b, s, Q_WIDTH), k_pb.reshape(b, s, K_WIDTH), v_pb.reshape(b, s, V_WIDTH),
                               lam_vecs, g_subln[l], lam_init).reshape(n_p, d)
        across_p = _cross(qx_p.reshape(b, s, d), mk3, mv3, 0, BF16).reshape(n_p, d)
        x1_p, h2_p, lg_p = _merge(pooled_p, adiff_p, across_p, gl_p, xp, wts)

        u_s, q_s, k_s, _, v_s, _, qx_s, gl_s = _rms_proj(xs, g_mix[l], w_in_bf, _in_splits(F32))
        pooled_s = _pool_sample(state_pool[l], u_s.reshape(db, t, d)).reshape(n_s, d)
        adiff_s = _attn_sample(q_s.reshape(db, t, Q_WIDTH), k_s.reshape(db, t, K_WIDTH), v_s.reshape(db, t, V_WIDTH),
                               cache_k3, cache_v3, page_table + l * n_phys, lam_vecs, g_subln[l],
                               lam_init).reshape(n_s, d)
        across_s = _cross(qx_s.reshape(db, t, d), cmk, cmv, l * db, F32).reshape(n_s, d)
        x1_s, h2_s, lg_s = _merge(pooled_s, adiff_s, across_s, gl_s, xs, wts)

        x1 = jnp.concatenate([x1_p, x1_s], axis=0)
        h2 = jnp.concatenate([h2_p, h2_s], axis=0)
        gate, dest, buf_tok, block_e, n_valid = _route(jnp.concatenate([lg_p, lg_s], axis=0))
        h2_pad = jnp.concatenate([h2, jnp.zeros((1, d), BF16)], axis=0)
        out_buf = _experts(h2_pad[buf_tok], block_e, n_valid, w_up[l].astype(BF16),
                           b_up[l].reshape(N_EXPERTS, 1, 2 * D_FF), w_down[l].astype(BF16),
                           b_down[l].reshape(N_EXPERTS, 1, d))
        rows = out_buf[dest.T]
        last = l == depth - 1
        g_out = g_final if last else jnp.ones((d,), F32)
        y = _final(x1, rows, gate, g_out)
        if not last:
            raise NotImplementedError("depth > 1 needs the un-normalised residual stream")
        xp, xs = y[:n_p], y[n_p:]

        outs["kp"].append(k_p.reshape(b, s, N_KV_HEADS, 2, HEAD_DIM))
        outs["vp"].append(v_p.reshape(b, s, N_KV_HEADS, V_DIM))
        outs["mk"].append(mk3.reshape(b, n_mem, X_HEADS, X_DIM))
        outs["mv"].append(mv3.reshape(b, n_mem, X_HEADS, X_DIM))
        outs["hp"].append(u_p.reshape(b, s, d)[:, -POOL_STATE:].astype(state_pool.dtype))
        outs["ks"].append(k_s.reshape(db, t, N_KV_HEADS, 2, HEAD_DIM))
        outs["vs"].append(v_s.reshape(db, t, N_KV_HEADS, V_DIM))
        outs["hs"].append(jnp.concatenate([state_pool[l], u_s.reshape(db, t, d).astype(state_pool.dtype)],
                                          axis=1)[:, -POOL_STATE:])

    return (xp.reshape(b, s, d), xs.reshape(db, t, d), jnp.stack(outs["kp"]), jnp.stack(outs["vp"]),
            jnp.stack(outs["mk"]), jnp.stack(outs["mv"]), jnp.stack(outs["hp"]), jnp.stack(outs["ks"]),
            jnp.stack(outs["vs"]), jnp.stack(outs["hs"]))
```

```python
import functools
import math

import numpy as np
import jax
import jax.numpy as jnp
from jax import lax
from jax.experimental import pallas as pl
from jax.experimental.pallas import tpu as pltpu

F32 = jnp.float32
BF16 = jnp.bfloat16

D_MODEL = 1024
N_HEADS = 8
HEAD_DIM = 64
N_KV_HEADS = 4
GQA_GROUP = N_HEADS // N_KV_HEADS
V_DIM = 2 * HEAD_DIM
POOL_WINDOWS = (2, 4, 8, 16)
POOL_GROUPS = 4
POOL_GROUP_DIM = D_MODEL // POOL_GROUPS
POOL_STATE = 15
POOL_HALO = 16
X_HEADS = 4
X_DIM = D_MODEL // X_HEADS
N_BRANCH = 3
N_EXPERTS = 32
TOP_K = 4
D_FF = D_MODEL
SWIGLU_LIMIT = 7.0
SWIGLU_ALPHA = 1.702
RMS_EPS = 1e-6
Q_WIDTH = N_HEADS * 2 * HEAD_DIM
K_WIDTH = N_KV_HEADS * 2 * HEAD_DIM
V_WIDTH = N_KV_HEADS * V_DIM
KV_BLOCK = 2 * HEAD_DIM
MASK_VALUE = -1e30

LANES = 128
EXPERT_LANES = LANES
MIB = 2 ** 20

ROW_TILE = 256
ATTN_TQ = 512
ATTN_TK = 512
PAGES_PER_STEP = 8
CROSS_TQ = 512
MOE_ROWS = 256
POOL_SEQ_BLOCK = 8


def _cparams(semantics, vmem_mib):
    return pltpu.CompilerParams(dimension_semantics=semantics, vmem_limit_bytes=vmem_mib * MIB)


def _const_spec(shape):
    zeros = (0,) * len(shape)
    return pl.BlockSpec(shape, lambda *_: zeros)


def _rms(xf, g):
    return xf * lax.rsqrt(jnp.mean(xf * xf, axis=-1, keepdims=True) + RMS_EPS) * g


PLAIN = "plain"
WITH_ONES = "with_ones"


def _rms_proj_kernel(x_ref, g_ref, w_ref, *out_refs, splits):
    hb = _rms(x_ref[...].astype(F32), g_ref[...]).astype(BF16)
    oi = 0
    for c0, width, scale, copies in splits:
        z = jnp.dot(hb, w_ref[:, c0:c0 + width], preferred_element_type=F32)
        if scale != 1.0:
            z = z * scale
        for dt, layout in copies:
            o_ref = out_refs[oi]
            oi += 1
            if layout == WITH_ONES:
                for h in range(width // V_DIM):
                    o_ref[:, 2 * h * V_DIM:(2 * h + 1) * V_DIM] = z[:, h * V_DIM:(h + 1) * V_DIM].astype(dt)
                    o_ref[:, (2 * h + 1) * V_DIM:(2 * h + 2) * V_DIM] = jnp.ones((z.shape[0], V_DIM), dt)
            else:
                o_ref[...] = z.astype(dt)


def _rms_proj(x2d, g, w_bf, splits):
    n, d = x2d.shape
    tm = ROW_TILE
    assert n % tm == 0
    out_shape, out_specs = [], []
    for _, width, _, copies in splits:
        for dt, layout in copies:
            w_out = 2 * width if layout == WITH_ONES else width
            out_shape.append(jax.ShapeDtypeStruct((n, w_out), dt))
            out_specs.append(pl.BlockSpec((tm, w_out), lambda i: (i, 0)))
    return pl.pallas_call(
        functools.partial(_rms_proj_kernel, splits=splits),
        grid=(n // tm,),
        in_specs=[pl.BlockSpec((tm, d), lambda i: (i, 0)), _const_spec((1, d)), _const_spec(w_bf.shape)],
        out_specs=out_specs,
        out_shape=out_shape,
        compiler_params=_cparams(("parallel",), 56),
        name="rms_proj",
    )(x2d, g.reshape(1, d), w_bf)


def _pool_prompt_kernel(u_ref, halo_ref, o_ref, ue, *, tm):
    i = pl.program_id(1)
    ue[0:POOL_HALO, :] = jnp.where(i == 0, 0.0, halo_ref[...])
    ue[POOL_HALO:, :] = u_ref[...]
    pos = i * tm + lax.broadcasted_iota(jnp.int32, (tm, 1), 0)
    for g, w in enumerate(POOL_WINDOWS):
        c = slice(g * POOL_GROUP_DIM, (g + 1) * POOL_GROUP_DIM)
        tok = ue[POOL_HALO:POOL_HALO + tm, c]
        acc = tok
        for j in range(1, w):
            acc = acc + ue[POOL_HALO - j:POOL_HALO - j + tm, c]
        cnt = jnp.minimum(pos + 1, w).astype(F32)
        o_ref[:, c] = (acc / cnt - tok).astype(o_ref.dtype)


def _pool_prompt(u3):
    b, s, d = u3.shape
    tm = ROW_TILE
    halo_blocks = tm // POOL_HALO
    return pl.pallas_call(
        functools.partial(_pool_prompt_kernel, tm=tm),
        grid=(b, s // tm),
        in_specs=[pl.BlockSpec((None, tm, d), lambda bi, i: (bi, i, 0)),
                  pl.BlockSpec((None, POOL_HALO, d), lambda bi, i: (bi, jnp.maximum(i * halo_blocks - 1, 0), 0))],
        out_specs=pl.BlockSpec((None, tm, d), lambda bi, i: (bi, i, 0)),
        out_shape=jax.ShapeDtypeStruct((b, s, d), BF16),
        scratch_shapes=[pltpu.VMEM((tm + POOL_HALO, d), F32)],
        compiler_params=_cparams(("parallel", "parallel"), 32),
        name="pool_prompt",
    )(u3, u3)


def _pool_sample_kernel(ue_ref, o_ref, *, n_new):
    lo = ue_ref.shape[1] - n_new
    for s in range(ue_ref.shape[0]):
        for g, w in enumerate(POOL_WINDOWS):
            c = slice(g * POOL_GROUP_DIM, (g + 1) * POOL_GROUP_DIM)
            tok = ue_ref[s, lo:lo + n_new, c]
            acc = tok
            for j in range(1, w):
                acc = acc + ue_ref[s, lo - j:lo - j + n_new, c]
            o_ref[s, :, c] = acc / float(w) - tok


def _pool_sample(hist, u3):
    db, t, d = u3.shape
    assert hist.shape[1] >= POOL_WINDOWS[-1] - 1
    pad = (-(hist.shape[1] + t)) % 8
    ue = jnp.concatenate([jnp.zeros((db, pad, d), F32), hist.astype(F32), u3], axis=1)
    rows = ue.shape[1]
    sb = POOL_SEQ_BLOCK
    return pl.pallas_call(
        functools.partial(_pool_sample_kernel, n_new=t),
        grid=(db // sb,),
        in_specs=[pl.BlockSpec((sb, rows, d), lambda i: (i, 0, 0))],
        out_specs=pl.BlockSpec((sb, t, d), lambda i: (i, 0, 0)),
        out_shape=jax.ShapeDtypeStruct((db, t, d), F32),
        compiler_params=_cparams(("parallel",), 32),
        name="pool_sample",
    )(ue)


def _lambda_value(lam_ref, lam_init):
    a = jnp.sum(lam_ref[0:1, :] * lam_ref[1:2, :], axis=-1, keepdims=True)
    b = jnp.sum(lam_ref[2:3, :] * lam_ref[3:4, :], axis=-1, keepdims=True)
    return jnp.exp(a) - jnp.exp(b) + lam_init


def _subln(o, gsub, lam_init):
    return _rms(o, gsub) * (1.0 - lam_init)


def _softmax_update(s, m_prev):
    m_new = jnp.maximum(m_prev, jnp.max(s, axis=-1, keepdims=True))
    alpha = jnp.exp2(m_prev - m_new)
    p = jnp.exp2(s - jnp.tile(m_new, (1, s.shape[1] // LANES)))
    return p, m_new, alpha


def _attn_prompt_kernel(qi_ref, ki_ref, last_ref, q_ref, k_ref, v_ref, lam_ref, gsub_ref, o_ref,
                        qpad, acc, m_scr, *, tq, tk, lam_init):
    step = pl.program_id(1)
    qi = qi_ref[step]
    ki = ki_ref[step]
    n_streams = 2 * N_KV_HEADS

    @pl.when(ki == 0)
    def _init():
        lane = lax.broadcasted_iota(jnp.int32, (tq, KV_BLOCK), 1)
        for h in range(N_KV_HEADS):
            for g in range(GQA_GROUP):
                c0 = (h * GQA_GROUP + g) * KV_BLOCK
                blk = q_ref[:, c0:c0 + KV_BLOCK]
                rows = slice(g * tq, (g + 1) * tq)
                qpad[2 * h, rows, :] = jnp.where(lane < HEAD_DIM, blk, jnp.zeros_like(blk))
                qpad[2 * h + 1, rows, :] = jnp.where(lane >= HEAD_DIM, blk, jnp.zeros_like(blk))
        acc[...] = jnp.zeros_like(acc)
        m_scr[...] = jnp.full_like(m_scr, MASK_VALUE)

    def scores(idx):
        h = idx // 2
        return lax.dot_general(qpad[idx], k_ref[:, h * KV_BLOCK:(h + 1) * KV_BLOCK], (((1,), (1,)), ((), ())),
                               preferred_element_type=F32)

    def body(masked):
        if masked:
            qpos = qi * tq + lax.broadcasted_iota(jnp.int32, (tq, tk), 0)
            kpos = ki * tk + lax.broadcasted_iota(jnp.int32, (tq, tk), 1)
            vis1 = kpos <= qpos
            vis = jnp.concatenate([vis1] * GQA_GROUP, axis=0)
        s_next = scores(0)
        for idx in range(n_streams):
            s = s_next
            if idx + 1 < n_streams:
                s_next = scores(idx + 1)
            if masked:
                s = jnp.where(vis, s, MASK_VALUE)
            p, m_new, alpha = _softmax_update(s, m_scr[idx])
            m_scr[idx] = m_new
            h = idx // 2
            pv = jnp.dot(p.astype(BF16), v_ref[:, h * 2 * V_DIM:(h + 1) * 2 * V_DIM], preferred_element_type=F32)
            acc[idx] = jnp.tile(alpha, (1, 2)) * acc[idx] + pv

    needs_mask = (ki + 1) * tk - 1 > qi * tq

    @pl.when(needs_mask)
    def _masked():
        body(True)

    @pl.when(jnp.logical_not(needs_mask))
    def _unmasked():
        body(False)

    @pl.when(last_ref[step] == 1)
    def _finish():
        lam = _lambda_value(lam_ref, lam_init)
        gsub = gsub_ref[...]
        for h in range(N_KV_HEADS):
            for g in range(GQA_GROUP):
                rows = slice(g * tq, (g + 1) * tq)
                o1 = acc[2 * h, rows, 0:V_DIM] / acc[2 * h, rows, V_DIM:2 * V_DIM]
                o2 = acc[2 * h + 1, rows, 0:V_DIM] / acc[2 * h + 1, rows, V_DIM:2 * V_DIM]
                c0 = (h * GQA_GROUP + g) * V_DIM
                o_ref[:, c0:c0 + V_DIM] = _subln(o1 - lam * o2, gsub, lam_init).astype(o_ref.dtype)


def _attn_prompt(q, k, v, lam_vecs, gsub, lam_init):
    b, s, _ = q.shape
    tq, tk = ATTN_TQ, ATTN_TK
    assert s % tq == 0 and s % tk == 0
    qi_l, ki_l, last_l = [], [], []
    for qi in range(s // tq):
        n_k = min(((qi + 1) * tq - 1) // tk + 1, s // tk)
        for ki in range(n_k):
            qi_l.append(qi)
            ki_l.append(ki)
            last_l.append(1 if ki == n_k - 1 else 0)
    tabs = [jnp.asarray(np.array(t, np.int32)) for t in (qi_l, ki_l, last_l)]
    rows = GQA_GROUP * tq
    grid_spec = pltpu.PrefetchScalarGridSpec(
        num_scalar_prefetch=3,
        grid=(b, len(qi_l)),
        in_specs=[pl.BlockSpec((None, tq, Q_WIDTH), lambda bi, st, qt, kt, lt: (bi, qt[st], 0)),
                  pl.BlockSpec((None, tk, K_WIDTH), lambda bi, st, qt, kt, lt: (bi, kt[st], 0)),
                  pl.BlockSpec((None, tk, 2 * V_WIDTH), lambda bi, st, qt, kt, lt: (bi, kt[st], 0)),
                  _const_spec((4, HEAD_DIM)), _const_spec((1, V_DIM))],
        out_specs=pl.BlockSpec((None, tq, N_HEADS * V_DIM), lambda bi, st, qt, kt, lt: (bi, qt[st], 0)),
        scratch_shapes=[pltpu.VMEM((2 * N_KV_HEADS, rows, KV_BLOCK), BF16),
                        pltpu.VMEM((2 * N_KV_HEADS, rows, 2 * V_DIM), F32),
                        pltpu.VMEM((2 * N_KV_HEADS, rows, LANES), F32)])
    return pl.pallas_call(
        functools.partial(_attn_prompt_kernel, tq=tq, tk=tk, lam_init=lam_init),
        grid_spec=grid_spec,
        out_shape=jax.ShapeDtypeStruct((b, s, N_HEADS * V_DIM), BF16),
        compiler_params=_cparams(("parallel", "arbitrary"), 56),
        name="attn_prompt",
    )(*tabs, q, k, v, lam_vecs, gsub.reshape(1, V_DIM))


SAMPLE_ROWS = N_KV_HEADS * 2 * GQA_GROUP


def _attn_sample_kernel(pt_ref, wq_ref, kn_ref, vn_ref, lam_ref, gsub_ref, *rest, n_pages, t_new, page, lam_init):
    k_refs = rest[:n_pages]
    v_refs = rest[n_pages:2 * n_pages]
    o_ref, acc, m_scr, l_scr = rest[2 * n_pages:]
    step = pl.program_id(1)
    rows = SAMPLE_ROWS * t_new

    @pl.when(step == 0)
    def _init():
        acc[...] = jnp.zeros_like(acc)
        m_scr[...] = jnp.full_like(m_scr, MASK_VALUE)
        l_scr[...] = jnp.zeros_like(l_scr)

    wq = wq_ref[...]

    def update(s, v_list):
        p, m_new, alpha = _softmax_update(s, m_scr[...])
        m_scr[...] = m_new
        l_scr[...] = alpha * l_scr[...] + jnp.sum(p, axis=-1, keepdims=True)
        pb = p.astype(BF16)
        pv = None
        for j, vj in enumerate(v_list):
            c = vj.shape[0]
            term = jnp.dot(pb[:, j * c:(j + 1) * c], vj, preferred_element_type=F32)
            pv = term if pv is None else pv + term
        acc[...] = jnp.tile(alpha, (1, N_KV_HEADS)) * acc[...] + pv

    def page_values(v_ref):
        heads = [v_ref[pl.ds(h, page, stride=N_KV_HEADS), :] for h in range(N_KV_HEADS)]
        return jnp.concatenate(heads, axis=1).astype(BF16)

    s_list = [jnp.dot(wq, kr[...].astype(BF16), preferred_element_type=F32) for kr in k_refs]
    update(jnp.concatenate(s_list, axis=1), [page_values(vr) for vr in v_refs])

    @pl.when(step == pl.num_programs(1) - 1)
    def _finish():
        pad = jnp.zeros((LANES - t_new, kn_ref.shape[1]), F32)
        kn = jnp.concatenate([kn_ref[...], pad], axis=0).astype(BF16)
        vn = jnp.concatenate([vn_ref[...], pad], axis=0).astype(BF16)
        s_new = lax.dot_general(wq, kn, (((1,), (1,)), ((), ())), preferred_element_type=F32)
        t_of_row = lax.broadcasted_iota(jnp.int32, (rows, LANES), 0) % t_new
        key = lax.broadcasted_iota(jnp.int32, (rows, LANES), 1)
        update(jnp.where(key <= t_of_row, s_new, MASK_VALUE), [vn])

        lam = _lambda_value(lam_ref, lam_init)
        gsub = gsub_ref[...]
        for h in range(N_KV_HEADS):
            for g in range(GQA_GROUP):
                r1 = (h * 2 * GQA_GROUP + g) * t_new
                r2 = (h * 2 * GQA_GROUP + GQA_GROUP + g) * t_new
                cols = slice(h * V_DIM, (h + 1) * V_DIM)
                o1 = acc[r1:r1 + t_new, cols] / l_scr[r1:r1 + t_new, :]
                o2 = acc[r2:r2 + t_new, cols] / l_scr[r2:r2 + t_new, :]
                c0 = (h * GQA_GROUP + g) * V_DIM
                o_ref[:, c0:c0 + V_DIM] = _subln(o1 - lam * o2, gsub, lam_init).astype(o_ref.dtype)


def _sample_query_rows(q3):
    db, t, _ = q3.shape
    q6 = q3.reshape(db, t, N_KV_HEADS, GQA_GROUP, 2, HEAD_DIM).transpose(0, 2, 4, 3, 1, 5)
    eye = jnp.eye(N_KV_HEADS * 2, dtype=q3.dtype).reshape(N_KV_HEADS, 2, N_KV_HEADS, 2)
    wq = jnp.einsum("bhmgtd,hmHM->bhmgtHMd", q6, eye)
    return wq.reshape(db, SAMPLE_ROWS * t, K_WIDTH).astype(BF16)


def _attn_sample(q3, k_new, v_new, cache_kt, cache_v, page, page_ids, lam_vecs, gsub, lam_init):
    db, t, _ = q3.shape
    n_pages_total = page_ids.shape[1]
    g = PAGES_PER_STEP
    rows = SAMPLE_ROWS * t
    assert n_pages_total % g == 0 and rows % 8 == 0 and t <= LANES and page % LANES == 0
    wq = _sample_query_rows(q3)

    def k_spec(j):
        return pl.BlockSpec((K_WIDTH, page), lambda bi, st, pt: (pt[bi, st * g + j], 0))

    def v_spec(j):
        return pl.BlockSpec((page * N_KV_HEADS, V_DIM), lambda bi, st, pt: (pt[bi, st * g + j], 0))

    grid_spec = pltpu.PrefetchScalarGridSpec(
        num_scalar_prefetch=1,
        grid=(db, n_pages_total // g),
        in_specs=[pl.BlockSpec((None, rows, K_WIDTH), lambda bi, st, pt: (bi, 0, 0)),
                  pl.BlockSpec((None, t, K_WIDTH), lambda bi, st, pt: (bi, 0, 0)),
                  pl.BlockSpec((None, t, V_WIDTH), lambda bi, st, pt: (bi, 0, 0)),
                  _const_spec((4, HEAD_DIM)), _const_spec((1, V_DIM))]
                 + [k_spec(j) for j in range(g)] + [v_spec(j) for j in range(g)],
        out_specs=pl.BlockSpec((None, t, N_HEADS * V_DIM), lambda bi, st, pt: (bi, 0, 0)),
        scratch_shapes=[pltpu.VMEM((rows, V_WIDTH), F32),
                        pltpu.VMEM((rows, LANES), F32),
                        pltpu.VMEM((rows, LANES), F32)])
    return pl.pallas_call(
        functools.partial(_attn_sample_kernel, n_pages=g, t_new=t, page=page, lam_init=lam_init),
        grid_spec=grid_spec,
        out_shape=jax.ShapeDtypeStruct((db, t, N_HEADS * V_DIM), F32),
        compiler_params=_cparams(("parallel", "arbitrary"), 40),
        name="attn_sample",
    )(page_ids, wq, k_new, v_new, lam_vecs, gsub.reshape(1, V_DIM), *([cache_kt] * g), *([cache_v] * g))


def _cross_kernel(q_ref, mk_ref, mv_ref, o_ref):
    for h in range(X_HEADS):
        c = slice(h * X_DIM, (h + 1) * X_DIM)
        q = q_ref[:, c].astype(BF16)
        s = lax.dot_general(q, mk_ref[:, c].astype(BF16), (((1,), (1,)), ((), ())), preferred_element_type=F32)
        p = jnp.exp(s - jnp.max(s, axis=-1, keepdims=True))
        l = jnp.sum(p, axis=-1, keepdims=True)
        o = jnp.dot(p.astype(BF16), mv_ref[:, c].astype(BF16), preferred_element_type=F32)
        o_ref[:, c] = (o / l).astype(o_ref.dtype)


def _cross(q3, mk, mv, mem_offset, out_dtype):
    b, t, d = q3.shape
    tq = min(CROSS_TQ, t)
    n_mem = mk.shape[1]
    return pl.pallas_call(
        _cross_kernel,
        grid=(b, t // tq),
        in_specs=[pl.BlockSpec((None, tq, d), lambda bi, i: (bi, i, 0)),
                  pl.BlockSpec((None, n_mem, d), lambda bi, i: (bi + mem_offset, 0, 0)),
                  pl.BlockSpec((None, n_mem, d), lambda bi, i: (bi + mem_offset, 0, 0))],
        out_specs=pl.BlockSpec((None, tq, d), lambda bi, i: (bi, i, 0)),
        out_shape=jax.ShapeDtypeStruct((b, t, d), out_dtype),
        compiler_params=_cparams(("parallel", "parallel"), 32),
        name="cross_attn",
    )(q3, mk, mv)


def _merge_kernel(pooled_ref, adiff_ref, across_ref, gl_ref, x_ref, wg_ref, ps_ref, wpp_ref, wdp_ref, wcp_ref,
                  wo_ref, gffn_ref, wr_ref, br_ref, x1_ref, h2_ref, lg_ref):
    pooled = pooled_ref[...].astype(BF16)
    ys = [jnp.dot(pooled[:, g * POOL_GROUP_DIM:(g + 1) * POOL_GROUP_DIM], wg_ref[g], preferred_element_type=F32)
          for g in range(POOL_GROUPS)]
    a_pool = jnp.concatenate(ys, axis=1) * ps_ref[...]
    p_pool = jnp.dot(a_pool.astype(BF16), wpp_ref[...], preferred_element_type=F32)
    p_diff = jnp.dot(adiff_ref[...].astype(BF16), wdp_ref[...], preferred_element_type=F32)
    p_cross = jnp.dot(across_ref[...].astype(BF16), wcp_ref[...], preferred_element_type=F32)
    merged = (jax.nn.sigmoid(gl_ref[:, 0:D_MODEL]) * p_pool
              + jax.nn.sigmoid(gl_ref[:, D_MODEL:2 * D_MODEL]) * p_diff
              + jax.nn.sigmoid(gl_ref[:, 2 * D_MODEL:3 * D_MODEL]) * p_cross)
    x1 = x_ref[...] + jnp.dot(merged.astype(BF16), wo_ref[...], preferred_element_type=F32)
    x1_ref[...] = x1
    h2 = _rms(x1, gffn_ref[...]).astype(BF16)
    h2_ref[...] = h2
    lg_ref[...] = jnp.dot(h2, wr_ref[...], preferred_element_type=F32) + br_ref[...]


def _merge(pooled, adiff, across, gl, x2d, wts):
    n, d = x2d.shape
    tm = ROW_TILE
    row = lambda w: pl.BlockSpec((tm, w), lambda i: (i, 0))
    consts = [wts["wg"], wts["pool_scale"], wts["wpp"], wts["wdp"], wts["wcp"], wts["wo"], wts["g_ffn"],
              wts["wr"], wts["br"]]
    return pl.pallas_call(
        _merge_kernel,
        grid=(n // tm,),
        in_specs=[row(d), row(d), row(d), row(N_BRANCH * d), row(d)] + [_const_spec(c.shape) for c in consts],
        out_specs=[row(d), row(d), row(EXPERT_LANES)],
        out_shape=[jax.ShapeDtypeStruct((n, d), F32), jax.ShapeDtypeStruct((n, d), BF16),
                   jax.ShapeDtypeStruct((n, EXPERT_LANES), F32)],
        compiler_params=_cparams(("parallel",), 48),
        name="merge",
    )(pooled, adiff, across, gl, x2d, *consts)


def _expert_kernel(be_ref, nv_ref, xs_ref, wu_ref, bu_ref, wd_ref, bd_ref, o_ref):
    i = pl.program_id(0)

    @pl.when(i < nv_ref[0])
    def _run():
        up = jnp.dot(xs_ref[...], wu_ref[...], preferred_element_type=F32) + bu_ref[...]
        glu = jnp.minimum(up[:, :D_FF], SWIGLU_LIMIT)
        lin = jnp.clip(up[:, D_FF:], -SWIGLU_LIMIT, SWIGLU_LIMIT)
        act = glu * jax.nn.sigmoid(SWIGLU_ALPHA * glu) * (lin + 1.0)
        o_ref[...] = jnp.dot(act.astype(BF16), wd_ref[...], preferred_element_type=F32) + bd_ref[...]

    @pl.when(i >= nv_ref[0])
    def _skip():
        o_ref[...] = jnp.zeros_like(o_ref)


def _experts(xs, block_e, n_valid, wu, bu, wd, bd):
    rows, d = xs.shape
    tm = MOE_ROWS
    grid_spec = pltpu.PrefetchScalarGridSpec(
        num_scalar_prefetch=2,
        grid=(rows // tm,),
        in_specs=[pl.BlockSpec((tm, d), lambda i, be, nv: (i, 0)),
                  pl.BlockSpec((None, d, 2 * D_FF), lambda i, be, nv: (be[i], 0, 0)),
                  pl.BlockSpec((None, 1, 2 * D_FF), lambda i, be, nv: (be[i], 0, 0)),
                  pl.BlockSpec((None, D_FF, d), lambda i, be, nv: (be[i], 0, 0)),
                  pl.BlockSpec((None, 1, d), lambda i, be, nv: (be[i], 0, 0))],
        out_specs=pl.BlockSpec((tm, d), lambda i, be, nv: (i, 0)))
    return pl.pallas_call(
        _expert_kernel,
        grid_spec=grid_spec,
        out_shape=jax.ShapeDtypeStruct((rows, d), F32),
        compiler_params=_cparams(("arbitrary",), 48),
        name="experts",
    )(block_e, n_valid, xs, wu, bu, wd, bd)


def _final_kernel(x1_ref, rows_ref, gate_ref, gf_ref, y_ref):
    gate = gate_ref[...]
    moe = gate[:, 0:1] * rows_ref[0]
    for k in range(1, TOP_K):
        moe = moe + gate[:, k:k + 1] * rows_ref[k]
    y_ref[...] = _rms(x1_ref[...] + moe, gf_ref[...])


def _final(x1, rows, gate, g_final):
    n, d = x1.shape
    tm = ROW_TILE
    return pl.pallas_call(
        _final_kernel,
        grid=(n // tm,),
        in_specs=[pl.BlockSpec((tm, d), lambda i: (i, 0)),
                  pl.BlockSpec((TOP_K, tm, d), lambda i: (0, i, 0)),
                  pl.BlockSpec((tm, TOP_K), lambda i: (i, 0)),
                  _const_spec((1, d))],
        out_specs=pl.BlockSpec((tm, d), lambda i: (i, 0)),
        out_shape=jax.ShapeDtypeStruct((n, d), F32),
        compiler_params=_cparams(("parallel",), 32),
        name="final_norm",
    )(x1, rows, gate, g_final.reshape(1, d))


def _route(logits):
    n = logits.shape[0]
    top_v, top_e = lax.top_k(logits[:, :N_EXPERTS], TOP_K)
    gate = jax.nn.softmax(top_v, axis=-1)
    nk = n * TOP_K
    flat_e = top_e.reshape(nk)
    order = jnp.argsort(flat_e)
    se = flat_e[order]
    counts = jnp.bincount(flat_e, length=N_EXPERTS).astype(jnp.int32)
    padded = (counts + MOE_ROWS - 1) // MOE_ROWS * MOE_ROWS
    g_start = jnp.cumsum(counts) - counts
    p_end = jnp.cumsum(padded)
    p_start = p_end - padded
    dest_sorted = p_start[se] + jnp.arange(nk, dtype=jnp.int32) - g_start[se]
    n_blocks = -(-(nk + N_EXPERTS * (MOE_ROWS - 1)) // MOE_ROWS)
    buf_tok = jnp.full((n_blocks * MOE_ROWS,), n, jnp.int32).at[dest_sorted].set((order // TOP_K).astype(jnp.int32))
    dest = jnp.zeros((nk,), jnp.int32).at[order].set(dest_sorted).reshape(n, TOP_K)
    block_e = jnp.minimum(jnp.searchsorted(p_end, jnp.arange(n_blocks, dtype=jnp.int32) * MOE_ROWS, side="right"),
                          N_EXPERTS - 1).astype(jnp.int32)
    n_valid = (p_end[-1] // MOE_ROWS).astype(jnp.int32).reshape(1)
    return gate, dest, buf_tok, block_e, n_valid


def _in_splits(dt):
    off = np.cumsum([0, D_MODEL, Q_WIDTH, K_WIDTH, V_WIDTH, D_MODEL, N_BRANCH * D_MODEL])
    q_scale = HEAD_DIM ** -0.5 * math.log2(math.e)
    return ((int(off[0]), D_MODEL, 1.0, ((F32, PLAIN),)),
            (int(off[1]), Q_WIDTH, q_scale, ((dt, PLAIN),)),
            (int(off[2]), K_WIDTH, 1.0, ((F32, PLAIN), (BF16, PLAIN))),
            (int(off[3]), V_WIDTH, 1.0, ((F32, PLAIN), (BF16, WITH_ONES))),
            (int(off[4]), D_MODEL, X_DIM ** -0.5, ((dt, PLAIN),)),
            (int(off[5]), N_BRANCH * D_MODEL, 1.0, ((F32, PLAIN),)))


def kernel(x_prompt, x_sample, mem_prompt, cache_k, cache_v, cache_mem_k, cache_mem_v, state_pool, page_table, g_mix, w_in, w_pool_group, pool_scale, lambda_q1, lambda_k1, lambda_q2, lambda_k2, g_subln, g_mem, w_mem_kv, w_pool_proj, w_diff_proj, w_cross_proj, w_o, g_ffn, w_router, b_router, w_up, b_up, w_down, b_down, g_final):
    b, s, d = x_prompt.shape
    db, t, _ = x_sample.shape
    depth = w_in.shape[0]
    assert depth == 1, "the step is written for the single-layer trunk this problem states"
    n_phys, page = cache_k.shape[1], cache_k.shape[2]
    n_mem = mem_prompt.shape[1]
    n_p, n_s = b * s, db * t

    cache_kt = cache_k.transpose(0, 1, 3, 4, 5, 2).reshape(depth * n_phys * K_WIDTH, page)
    cache_v2 = cache_v.reshape(depth * n_phys * page * N_KV_HEADS, V_DIM)
    cmk = cache_mem_k.reshape(depth * db, n_mem, d)
    cmv = cache_mem_v.reshape(depth * db, n_mem, d)

    xp = x_prompt.reshape(n_p, d)
    xs = x_sample.reshape(n_s, d)
    outs = {name: [] for name in ("kp", "vp", "mk", "mv", "hp", "ks", "vs", "hs")}

    for l in range(depth):
        lam_init = 0.8 - 0.6 * math.exp(-0.3 * l)
        lam_vecs = jnp.stack([lambda_q1[l], lambda_k1[l], lambda_q2[l], lambda_k2[l]]).astype(F32)
        w_in_bf = w_in[l].astype(BF16)
        wr = jnp.zeros((d, EXPERT_LANES), BF16).at[:, :N_EXPERTS].set(w_router[l].astype(BF16))
        br = jnp.full((1, EXPERT_LANES), MASK_VALUE, F32).at[0, :N_EXPERTS].set(b_router[l].astype(F32))
        wts = dict(wg=w_pool_group[l].astype(BF16), pool_scale=pool_scale[l].reshape(1, d).astype(F32),
                   wpp=w_pool_proj[l].astype(BF16), wdp=w_diff_proj[l].astype(BF16),
                   wcp=w_cross_proj[l].astype(BF16), wo=w_o[l].astype(BF16),
                   g_ffn=g_ffn[l].reshape(1, d).astype(F32), wr=wr, br=br)

        mk, mv = _rms_proj(mem_prompt.reshape(b * n_mem, d), g_mem[l], w_mem_kv[l].astype(BF16),
                           ((0, d, 1.0, ((F32, PLAIN),)), (d, d, 1.0, ((F32, PLAIN),))))
        mk3, mv3 = mk.reshape(b, n_mem, d), mv.reshape(b, n_mem, d)

        u_p, q_p, k_p, k_pb, v_p, v_pb, qx_p, gl_p = _rms_proj(xp, g_mix[l], w_in_bf, _in_splits(BF16))
        pooled_p = _pool_prompt(u_p.reshape(b, s, d)).reshape(n_p, d)
        adiff_p = _attn_prompt(q_p.reshape(b, s, Q_WIDTH), k_pb.reshape(b, s, K_WIDTH),
                               v_pb.reshape(b, s, 2 * V_WIDTH), lam_vecs, g_subln[l], lam_init).reshape(n_p, d)
        across_p = _cross(qx_p.reshape(b, s, d), mk3, mv3, 0, BF16).reshape(n_p, d)
        x1_p, h2_p, lg_p = _merge(pooled_p, adiff_p, across_p, gl_p, xp, wts)

        u_s, q_s, k_s, _, v_s, _, qx_s, gl_s = _rms_proj(xs, g_mix[l], w_in_bf, _in_splits(F32))
        pooled_s = _pool_sample(state_pool[l], u_s.reshape(db, t, d)).reshape(n_s, d)
        adiff_s = _attn_sample(q_s.reshape(db, t, Q_WIDTH), k_s.reshape(db, t, K_WIDTH), v_s.reshape(db, t, V_WIDTH),
                               cache_kt, cache_v2, page, page_table + l * n_phys, lam_vecs, g_subln[l],
                               lam_init).reshape(n_s, d)
        across_s = _cross(qx_s.reshape(db, t, d), cmk, cmv, l * db, F32).reshape(n_s, d)
        x1_s, h2_s, lg_s = _merge(pooled_s, adiff_s, across_s, gl_s, xs, wts)

        x1 = jnp.concatenate([x1_p, x1_s], axis=0)
        h2 = jnp.concatenate([h2_p, h2_s], axis=0)
        gate, dest, buf_tok, block_e, n_valid = _route(jnp.concatenate([lg_p, lg_s], axis=0))
        h2_pad = jnp.concatenate([h2, jnp.zeros((1, d), BF16)], axis=0)
        out_buf = _experts(h2_pad[buf_tok], block_e, n_valid, w_up[l].astype(BF16),
                           b_up[l].reshape(N_EXPERTS, 1, 2 * D_FF), w_down[l].astype(BF16),
                           b_down[l].reshape(N_EXPERTS, 1, d))
        rows = out_buf[dest.T]
        y = _final(x1, rows, gate, g_final)
        xp, xs = y[:n_p], y[n_p:]

        outs["kp"].append(k_p.reshape(b, s, N_KV_HEADS, 2, HEAD_DIM))
        outs["vp"].append(v_p.reshape(b, s, N_KV_HEADS, V_DIM))
        outs["mk"].append(mk3.reshape(b, n_mem, X_HEADS, X_DIM))
        outs["mv"].append(mv3.reshape(b, n_mem, X_HEADS, X_DIM))
        outs["hp"].append(u_p.reshape(b, s, d)[:, -POOL_STATE:].astype(state_pool.dtype))
        outs["ks"].append(k_s.reshape(db, t, N_KV_HEADS, 2, HEAD_DIM))
        outs["vs"].append(v_s.reshape(db, t, N_KV_HEADS, V_DIM))
        outs["hs"].append(jnp.concatenate([state_pool[l], u_s.reshape(db, t, d).astype(state_pool.dtype)],
                                          axis=1)[:, -POOL_STATE:])

    return (xp.reshape(b, s, d), xs.reshape(db, t, d), jnp.stack(outs["kp"]), jnp.stack(outs["vp"]),
            jnp.stack(outs["mk"]), jnp.stack(outs["mv"]), jnp.stack(outs["hp"]), jnp.stack(outs["ks"]),
            jnp.stack(outs["vs"]), jnp.stack(outs["hs"]))
```

```python
import functools
import math

import numpy as np
import jax
import jax.numpy as jnp
from jax import lax
from jax.experimental import pallas as pl
from jax.experimental.pallas import tpu as pltpu

F32 = jnp.float32
BF16 = jnp.bfloat16

D_MODEL = 1024
N_HEADS = 8
HEAD_DIM = 64
N_KV_HEADS = 4
GQA_GROUP = N_HEADS // N_KV_HEADS
V_DIM = 2 * HEAD_DIM
POOL_WINDOWS = (2, 4, 8, 16)
POOL_GROUPS = 4
POOL_GROUP_DIM = D_MODEL // POOL_GROUPS
POOL_STATE = 15
POOL_HALO = 16
X_HEADS = 4
X_DIM = D_MODEL // X_HEADS
N_BRANCH = 3
N_EXPERTS = 32
TOP_K = 4
D_FF = D_MODEL
SWIGLU_LIMIT = 7.0
SWIGLU_ALPHA = 1.702
RMS_EPS = 1e-6
Q_WIDTH = N_HEADS * 2 * HEAD_DIM
K_WIDTH = N_KV_HEADS * 2 * HEAD_DIM
V_WIDTH = N_KV_HEADS * V_DIM
KV_BLOCK = 2 * HEAD_DIM
MASK_VALUE = -1e30

LANES = 128
EXPERT_LANES = LANES
MIB = 2 ** 20

ROW_TILE = 256
ATTN_TQ = 512
ATTN_TK = 512
PAGES_PER_STEP = 16
CROSS_TQ = 512
MOE_ROWS = 256
POOL_SEQ_BLOCK = 8


def _cparams(semantics, vmem_mib):
    return pltpu.CompilerParams(dimension_semantics=semantics, vmem_limit_bytes=vmem_mib * MIB)


def _const_spec(shape):
    zeros = (0,) * len(shape)
    return pl.BlockSpec(shape, lambda *_: zeros)


def _rms(xf, g):
    return xf * lax.rsqrt(jnp.mean(xf * xf, axis=-1, keepdims=True) + RMS_EPS) * g


TOKEN_TILE_ROWS = D_MODEL // LANES


def _store_token_tiles(ref, x):
    n = x.shape[0]
    for s in range(TOKEN_TILE_ROWS):
        ref[pl.ds(s, n, stride=TOKEN_TILE_ROWS), :] = x[:, s * LANES:(s + 1) * LANES]


def _load_token_tiles(ref, n, lead=()):
    return [ref[lead + (pl.ds(s, n, stride=TOKEN_TILE_ROWS), slice(None))] for s in range(TOKEN_TILE_ROWS)]


PLAIN = "plain"
WITH_ONES = "with_ones"


def _rms_proj_kernel(x_ref, g_ref, w_ref, *out_refs, splits):
    hb = _rms(x_ref[...].astype(F32), g_ref[...]).astype(BF16)
    oi = 0
    for c0, width, scale, copies in splits:
        z = jnp.dot(hb, w_ref[:, c0:c0 + width], preferred_element_type=F32)
        if scale != 1.0:
            z = z * scale
        for dt, layout in copies:
            o_ref = out_refs[oi]
            oi += 1
            if layout == WITH_ONES:
                for h in range(width // V_DIM):
                    o_ref[:, 2 * h * V_DIM:(2 * h + 1) * V_DIM] = z[:, h * V_DIM:(h + 1) * V_DIM].astype(dt)
                    o_ref[:, (2 * h + 1) * V_DIM:(2 * h + 2) * V_DIM] = jnp.ones((z.shape[0], V_DIM), dt)
            else:
                o_ref[...] = z.astype(dt)


def _rms_proj(x2d, g, w_bf, splits):
    n, d = x2d.shape
    tm = ROW_TILE
    assert n % tm == 0
    out_shape, out_specs = [], []
    for _, width, _, copies in splits:
        for dt, layout in copies:
            w_out = 2 * width if layout == WITH_ONES else width
            out_shape.append(jax.ShapeDtypeStruct((n, w_out), dt))
            out_specs.append(pl.BlockSpec((tm, w_out), lambda i: (i, 0)))
    return pl.pallas_call(
        functools.partial(_rms_proj_kernel, splits=splits),
        grid=(n // tm,),
        in_specs=[pl.BlockSpec((tm, d), lambda i: (i, 0)), _const_spec((1, d)), _const_spec(w_bf.shape)],
        out_specs=out_specs,
        out_shape=out_shape,
        compiler_params=_cparams(("parallel",), 56),
        name="rms_proj",
    )(x2d, g.reshape(1, d), w_bf)


def _pool_prompt_kernel(u_ref, halo_ref, o_ref, ue, *, tm):
    i = pl.program_id(1)
    ue[0:POOL_HALO, :] = jnp.where(i == 0, 0.0, halo_ref[...])
    ue[POOL_HALO:, :] = u_ref[...]
    pos = i * tm + lax.broadcasted_iota(jnp.int32, (tm, 1), 0)
    for g, w in enumerate(POOL_WINDOWS):
        c = slice(g * POOL_GROUP_DIM, (g + 1) * POOL_GROUP_DIM)
        tok = ue[POOL_HALO:POOL_HALO + tm, c]
        acc = tok
        for j in range(1, w):
            acc = acc + ue[POOL_HALO - j:POOL_HALO - j + tm, c]
        cnt = jnp.minimum(pos + 1, w).astype(F32)
        o_ref[:, c] = (acc / cnt - tok).astype(o_ref.dtype)


def _pool_prompt(u3):
    b, s, d = u3.shape
    tm = ROW_TILE
    halo_blocks = tm // POOL_HALO
    return pl.pallas_call(
        functools.partial(_pool_prompt_kernel, tm=tm),
        grid=(b, s // tm),
        in_specs=[pl.BlockSpec((None, tm, d), lambda bi, i: (bi, i, 0)),
                  pl.BlockSpec((None, POOL_HALO, d), lambda bi, i: (bi, jnp.maximum(i * halo_blocks - 1, 0), 0))],
        out_specs=pl.BlockSpec((None, tm, d), lambda bi, i: (bi, i, 0)),
        out_shape=jax.ShapeDtypeStruct((b, s, d), BF16),
        scratch_shapes=[pltpu.VMEM((tm + POOL_HALO, d), F32)],
        compiler_params=_cparams(("parallel", "parallel"), 32),
        name="pool_prompt",
    )(u3, u3)


def _pool_sample_kernel(ue_ref, o_ref, *, n_new):
    lo = ue_ref.shape[1] - n_new
    for s in range(ue_ref.shape[0]):
        for g, w in enumerate(POOL_WINDOWS):
            c = slice(g * POOL_GROUP_DIM, (g + 1) * POOL_GROUP_DIM)
            tok = ue_ref[s, lo:lo + n_new, c]
            acc = tok
            for j in range(1, w):
                acc = acc + ue_ref[s, lo - j:lo - j + n_new, c]
            o_ref[s, :, c] = acc / float(w) - tok


def _pool_sample(hist, u3):
    db, t, d = u3.shape
    assert hist.shape[1] >= POOL_WINDOWS[-1] - 1
    pad = (-(hist.shape[1] + t)) % 8
    ue = jnp.concatenate([jnp.zeros((db, pad, d), F32), hist.astype(F32), u3], axis=1)
    rows = ue.shape[1]
    sb = POOL_SEQ_BLOCK
    return pl.pallas_call(
        functools.partial(_pool_sample_kernel, n_new=t),
        grid=(db // sb,),
        in_specs=[pl.BlockSpec((sb, rows, d), lambda i: (i, 0, 0))],
        out_specs=pl.BlockSpec((sb, t, d), lambda i: (i, 0, 0)),
        out_shape=jax.ShapeDtypeStruct((db, t, d), F32),
        compiler_params=_cparams(("parallel",), 32),
        name="pool_sample",
    )(ue)


def _lambda_value(lam_ref, lam_init):
    a = jnp.sum(lam_ref[0:1, :] * lam_ref[1:2, :], axis=-1, keepdims=True)
    b = jnp.sum(lam_ref[2:3, :] * lam_ref[3:4, :], axis=-1, keepdims=True)
    return jnp.exp(a) - jnp.exp(b) + lam_init


def _subln(o, gsub, lam_init):
    return _rms(o, gsub) * (1.0 - lam_init)


def _softmax_update(s, m_prev):
    m_new = jnp.maximum(m_prev, jnp.max(s, axis=-1, keepdims=True))
    alpha = jnp.exp2(m_prev - m_new)
    p = jnp.exp2(s - jnp.tile(m_new, (1, s.shape[1] // LANES)))
    return p, m_new, alpha


def _attn_prompt_kernel(qi_ref, ki_ref, last_ref, q_ref, k_ref, v_ref, lam_ref, gsub_ref, o_ref,
                        qpad, acc, m_scr, *, tq, tk, lam_init):
    step = pl.program_id(1)
    qi = qi_ref[step]
    ki = ki_ref[step]
    n_streams = 2 * N_KV_HEADS

    @pl.when(ki == 0)
    def _init():
        lane = lax.broadcasted_iota(jnp.int32, (tq, KV_BLOCK), 1)
        for h in range(N_KV_HEADS):
            for g in range(GQA_GROUP):
                c0 = (h * GQA_GROUP + g) * KV_BLOCK
                blk = q_ref[:, c0:c0 + KV_BLOCK]
                rows = slice(g * tq, (g + 1) * tq)
                qpad[2 * h, rows, :] = jnp.where(lane < HEAD_DIM, blk, jnp.zeros_like(blk))
                qpad[2 * h + 1, rows, :] = jnp.where(lane >= HEAD_DIM, blk, jnp.zeros_like(blk))
        acc[...] = jnp.zeros_like(acc)
        m_scr[...] = jnp.full_like(m_scr, MASK_VALUE)

    def scores(idx):
        h = idx // 2
        return lax.dot_general(qpad[idx], k_ref[:, h * KV_BLOCK:(h + 1) * KV_BLOCK], (((1,), (1,)), ((), ())),
                               preferred_element_type=F32)

    def body(masked):
        if masked:
            qpos = qi * tq + lax.broadcasted_iota(jnp.int32, (tq, tk), 0)
            kpos = ki * tk + lax.broadcasted_iota(jnp.int32, (tq, tk), 1)
            vis1 = kpos <= qpos
            vis = jnp.concatenate([vis1] * GQA_GROUP, axis=0)
        s_next = scores(0)
        for idx in range(n_streams):
            s = s_next
            if idx + 1 < n_streams:
                s_next = scores(idx + 1)
            if masked:
                s = jnp.where(vis, s, MASK_VALUE)
            p, m_new, alpha = _softmax_update(s, m_scr[idx])
            m_scr[idx] = m_new
            h = idx // 2
            pv = jnp.dot(p.astype(BF16), v_ref[:, h * 2 * V_DIM:(h + 1) * 2 * V_DIM], preferred_element_type=F32)
            acc[idx] = jnp.tile(alpha, (1, 2)) * acc[idx] + pv

    needs_mask = (ki + 1) * tk - 1 > qi * tq

    @pl.when(needs_mask)
    def _masked():
        body(True)

    @pl.when(jnp.logical_not(needs_mask))
    def _unmasked():
        body(False)

    @pl.when(last_ref[step] == 1)
    def _finish():
        lam = _lambda_value(lam_ref, lam_init)
        gsub = gsub_ref[...]
        for h in range(N_KV_HEADS):
            for g in range(GQA_GROUP):
                rows = slice(g * tq, (g + 1) * tq)
                o1 = acc[2 * h, rows, 0:V_DIM] / acc[2 * h, rows, V_DIM:2 * V_DIM]
                o2 = acc[2 * h + 1, rows, 0:V_DIM] / acc[2 * h + 1, rows, V_DIM:2 * V_DIM]
                c0 = (h * GQA_GROUP + g) * V_DIM
                o_ref[:, c0:c0 + V_DIM] = _subln(o1 - lam * o2, gsub, lam_init).astype(o_ref.dtype)


def _attn_prompt(q, k, v, lam_vecs, gsub, lam_init):
    b, s, _ = q.shape
    tq, tk = ATTN_TQ, ATTN_TK
    assert s % tq == 0 and s % tk == 0
    qi_l, ki_l, last_l = [], [], []
    for qi in range(s // tq):
        n_k = min(((qi + 1) * tq - 1) // tk + 1, s // tk)
        for ki in range(n_k):
            qi_l.append(qi)
            ki_l.append(ki)
            last_l.append(1 if ki == n_k - 1 else 0)
    tabs = [jnp.asarray(np.array(t, np.int32)) for t in (qi_l, ki_l, last_l)]
    rows = GQA_GROUP * tq
    grid_spec = pltpu.PrefetchScalarGridSpec(
        num_scalar_prefetch=3,
        grid=(b, len(qi_l)),
        in_specs=[pl.BlockSpec((None, tq, Q_WIDTH), lambda bi, st, qt, kt, lt: (bi, qt[st], 0)),
                  pl.BlockSpec((None, tk, K_WIDTH), lambda bi, st, qt, kt, lt: (bi, kt[st], 0)),
                  pl.BlockSpec((None, tk, 2 * V_WIDTH), lambda bi, st, qt, kt, lt: (bi, kt[st], 0)),
                  _const_spec((4, HEAD_DIM)), _const_spec((1, V_DIM))],
        out_specs=pl.BlockSpec((None, tq, N_HEADS * V_DIM), lambda bi, st, qt, kt, lt: (bi, qt[st], 0)),
        scratch_shapes=[pltpu.VMEM((2 * N_KV_HEADS, rows, KV_BLOCK), BF16),
                        pltpu.VMEM((2 * N_KV_HEADS, rows, 2 * V_DIM), F32),
                        pltpu.VMEM((2 * N_KV_HEADS, rows, LANES), F32)])
    return pl.pallas_call(
        functools.partial(_attn_prompt_kernel, tq=tq, tk=tk, lam_init=lam_init),
        grid_spec=grid_spec,
        out_shape=jax.ShapeDtypeStruct((b, s, N_HEADS * V_DIM), BF16),
        compiler_params=_cparams(("parallel", "arbitrary"), 56),
        name="attn_prompt",
    )(*tabs, q, k, v, lam_vecs, gsub.reshape(1, V_DIM))


SAMPLE_ROWS = N_KV_HEADS * 2 * GQA_GROUP


def _attn_sample_kernel(pt_ref, wq_ref, kn_ref, vn_ref, lam_ref, gsub_ref, kt_hbm, v_hbm, o_ref,
                        kbuf, vbuf, acc, m_scr, l_scr, sem_k, sem_v, *, n_pages, t_new, page, lam_init):
    seq = pl.program_id(0)
    step = pl.program_id(1)
    n_steps = pl.num_programs(1)
    rows = SAMPLE_ROWS * t_new
    k_rows = kbuf.shape[1] // n_pages
    v_rows = vbuf.shape[1] // n_pages
    flat = seq * n_steps + step
    slot = flat % 2

    def page_copies(seq_i, step_i, slot_i):
        copies = []
        for j in range(n_pages):
            pid = pt_ref[seq_i, step_i * n_pages + j]
            copies.append(pltpu.make_async_copy(kt_hbm.at[pl.ds(pl.multiple_of(pid * k_rows, k_rows), k_rows), :],
                                                kbuf.at[slot_i, pl.ds(j * k_rows, k_rows), :], sem_k.at[slot_i]))
            copies.append(pltpu.make_async_copy(v_hbm.at[pl.ds(pl.multiple_of(pid * v_rows, v_rows), v_rows), :],
                                                vbuf.at[slot_i, pl.ds(j * v_rows, v_rows), :], sem_v.at[slot_i]))
        return copies

    @pl.when(flat == 0)
    def _first_fetch():
        for c in page_copies(seq, step, slot):
            c.start()

    @pl.when(flat + 1 < pl.num_programs(0) * n_steps)
    def _prefetch():
        wrap = step + 1 == n_steps
        for c in page_copies(jnp.where(wrap, seq + 1, seq), jnp.where(wrap, 0, step + 1), 1 - slot):
            c.start()

    @pl.when(step == 0)
    def _init():
        acc[...] = jnp.zeros_like(acc)
        m_scr[...] = jnp.full_like(m_scr, MASK_VALUE)
        l_scr[...] = jnp.zeros_like(l_scr)

    pltpu.make_async_copy(kt_hbm.at[pl.ds(0, n_pages * k_rows), :], kbuf.at[slot], sem_k.at[slot]).wait()
    pltpu.make_async_copy(v_hbm.at[pl.ds(0, n_pages * v_rows), :], vbuf.at[slot], sem_v.at[slot]).wait()

    wq = wq_ref[...]

    def update(s, v_list):
        p, m_new, alpha = _softmax_update(s, m_scr[...])
        m_scr[...] = m_new
        l_scr[...] = alpha * l_scr[...] + jnp.sum(p, axis=-1, keepdims=True)
        pb = p.astype(BF16)
        pv = None
        for j, vj in enumerate(v_list):
            c = vj.shape[0]
            term = jnp.dot(pb[:, j * c:(j + 1) * c], vj, preferred_element_type=F32)
            pv = term if pv is None else pv + term
        acc[...] = jnp.tile(alpha, (1, N_KV_HEADS)) * acc[...] + pv

    def page_keys(j):
        return kbuf[slot, pl.ds(j * k_rows, k_rows), :].astype(BF16)

    def page_values(j):
        heads = [vbuf[slot, pl.ds(j * v_rows + h, page, stride=N_KV_HEADS), :] for h in range(N_KV_HEADS)]
        return jnp.concatenate(heads, axis=1).astype(BF16)

    s_list = [jnp.dot(wq, page_keys(j), preferred_element_type=F32) for j in range(n_pages)]
    update(jnp.concatenate(s_list, axis=1), [page_values(j) for j in range(n_pages)])

    @pl.when(step == pl.num_programs(1) - 1)
    def _finish():
        pad = jnp.zeros((LANES - t_new, kn_ref.shape[1]), F32)
        kn = jnp.concatenate([kn_ref[...], pad], axis=0).astype(BF16)
        vn = jnp.concatenate([vn_ref[...], pad], axis=0).astype(BF16)
        s_new = lax.dot_general(wq, kn, (((1,), (1,)), ((), ())), preferred_element_type=F32)
        t_of_row = lax.broadcasted_iota(jnp.int32, (rows, LANES), 0) % t_new
        key = lax.broadcasted_iota(jnp.int32, (rows, LANES), 1)
        update(jnp.where(key <= t_of_row, s_new, MASK_VALUE), [vn])

        lam = _lambda_value(lam_ref, lam_init)
        gsub = gsub_ref[...]
        for h in range(N_KV_HEADS):
            for g in range(GQA_GROUP):
                r1 = (h * 2 * GQA_GROUP + g) * t_new
                r2 = (h * 2 * GQA_GROUP + GQA_GROUP + g) * t_new
                cols = slice(h * V_DIM, (h + 1) * V_DIM)
                o1 = acc[r1:r1 + t_new, cols] / l_scr[r1:r1 + t_new, :]
                o2 = acc[r2:r2 + t_new, cols] / l_scr[r2:r2 + t_new, :]
                c0 = (h * GQA_GROUP + g) * V_DIM
                o_ref[:, c0:c0 + V_DIM] = _subln(o1 - lam * o2, gsub, lam_init).astype(o_ref.dtype)


def _sample_query_rows(q3):
    db, t, _ = q3.shape
    q6 = q3.reshape(db, t, N_KV_HEADS, GQA_GROUP, 2, HEAD_DIM).transpose(0, 2, 4, 3, 1, 5)
    eye = jnp.eye(N_KV_HEADS * 2, dtype=q3.dtype).reshape(N_KV_HEADS, 2, N_KV_HEADS, 2)
    wq = jnp.einsum("bhmgtd,hmHM->bhmgtHMd", q6, eye)
    return wq.reshape(db, SAMPLE_ROWS * t, K_WIDTH).astype(BF16)


def _attn_sample(q3, k_new, v_new, cache_kt, cache_v, page, page_ids, lam_vecs, gsub, lam_init):
    db, t, _ = q3.shape
    n_pages_total = page_ids.shape[1]
    g = PAGES_PER_STEP
    rows = SAMPLE_ROWS * t
    assert n_pages_total % g == 0 and rows % 8 == 0 and t <= LANES and page % LANES == 0
    wq = _sample_query_rows(q3)

    assert page == LANES and cache_kt.shape[1] == page and cache_v.shape[1] == V_DIM
    grid_spec = pltpu.PrefetchScalarGridSpec(
        num_scalar_prefetch=1,
        grid=(db, n_pages_total // g),
        in_specs=[pl.BlockSpec((None, rows, K_WIDTH), lambda bi, st, pt: (bi, 0, 0)),
                  pl.BlockSpec((None, t, K_WIDTH), lambda bi, st, pt: (bi, 0, 0)),
                  pl.BlockSpec((None, t, V_WIDTH), lambda bi, st, pt: (bi, 0, 0)),
                  _const_spec((4, HEAD_DIM)), _const_spec((1, V_DIM)),
                  pl.BlockSpec(memory_space=pl.ANY), pl.BlockSpec(memory_space=pl.ANY)],
        out_specs=pl.BlockSpec((None, t, N_HEADS * V_DIM), lambda bi, st, pt: (bi, 0, 0)),
        scratch_shapes=[pltpu.VMEM((2, g * K_WIDTH, page), F32),
                        pltpu.VMEM((2, g * page * N_KV_HEADS, V_DIM), F32),
                        pltpu.VMEM((rows, V_WIDTH), F32),
                        pltpu.VMEM((rows, LANES), F32),
                        pltpu.VMEM((rows, LANES), F32),
                        pltpu.SemaphoreType.DMA((2,)), pltpu.SemaphoreType.DMA((2,))])
    return pl.pallas_call(
        functools.partial(_attn_sample_kernel, n_pages=g, t_new=t, page=page, lam_init=lam_init),
        grid_spec=grid_spec,
        out_shape=jax.ShapeDtypeStruct((db, t, N_HEADS * V_DIM), F32),
        compiler_params=_cparams(("arbitrary", "arbitrary"), 40),
        name="attn_sample",
    )(page_ids, wq, k_new, v_new, lam_vecs, gsub.reshape(1, V_DIM), cache_kt, cache_v)


def _cross_kernel(q_ref, mk_ref, mv_ref, o_ref):
    for h in range(X_HEADS):
        c = slice(h * X_DIM, (h + 1) * X_DIM)
        q = q_ref[:, c].astype(BF16)
        s = lax.dot_general(q, mk_ref[:, c].astype(BF16), (((1,), (1,)), ((), ())), preferred_element_type=F32)
        p = jnp.exp(s - jnp.max(s, axis=-1, keepdims=True))
        l = jnp.sum(p, axis=-1, keepdims=True)
        o = jnp.dot(p.astype(BF16), mv_ref[:, c].astype(BF16), preferred_element_type=F32)
        o_ref[:, c] = (o / l).astype(o_ref.dtype)


def _cross(q3, mk, mv, mem_offset, out_dtype):
    b, t, d = q3.shape
    tq = min(CROSS_TQ, t)
    n_mem = mk.shape[1]
    return pl.pallas_call(
        _cross_kernel,
        grid=(b, t // tq),
        in_specs=[pl.BlockSpec((None, tq, d), lambda bi, i: (bi, i, 0)),
                  pl.BlockSpec((None, n_mem, d), lambda bi, i: (bi + mem_offset, 0, 0)),
                  pl.BlockSpec((None, n_mem, d), lambda bi, i: (bi + mem_offset, 0, 0))],
        out_specs=pl.BlockSpec((None, tq, d), lambda bi, i: (bi, i, 0)),
        out_shape=jax.ShapeDtypeStruct((b, t, d), out_dtype),
        compiler_params=_cparams(("parallel", "parallel"), 32),
        name="cross_attn",
    )(q3, mk, mv)


def _merge_kernel(pooled_ref, adiff_ref, across_ref, gl_ref, x_ref, wg_ref, ps_ref, wpp_ref, wdp_ref, wcp_ref,
                  wo_ref, gffn_ref, wr_ref, br_ref, x1_ref, h2_ref, lg_ref):
    pooled = pooled_ref[...].astype(BF16)
    ys = [jnp.dot(pooled[:, g * POOL_GROUP_DIM:(g + 1) * POOL_GROUP_DIM], wg_ref[g], preferred_element_type=F32)
          for g in range(POOL_GROUPS)]
    a_pool = jnp.concatenate(ys, axis=1) * ps_ref[...]
    p_pool = jnp.dot(a_pool.astype(BF16), wpp_ref[...], preferred_element_type=F32)
    p_diff = jnp.dot(adiff_ref[...].astype(BF16), wdp_ref[...], preferred_element_type=F32)
    p_cross = jnp.dot(across_ref[...].astype(BF16), wcp_ref[...], preferred_element_type=F32)
    merged = (jax.nn.sigmoid(gl_ref[:, 0:D_MODEL]) * p_pool
              + jax.nn.sigmoid(gl_ref[:, D_MODEL:2 * D_MODEL]) * p_diff
              + jax.nn.sigmoid(gl_ref[:, 2 * D_MODEL:3 * D_MODEL]) * p_cross)
    x1 = x_ref[...] + jnp.dot(merged.astype(BF16), wo_ref[...], preferred_element_type=F32)
    x1_ref[...] = x1
    h2 = _rms(x1, gffn_ref[...])
    _store_token_tiles(h2_ref, h2)
    lg_ref[...] = jnp.dot(h2.astype(BF16), wr_ref[...], preferred_element_type=F32) + br_ref[...]


def _merge(pooled, adiff, across, gl, x2d, wts):
    n, d = x2d.shape
    tm = ROW_TILE
    row = lambda w: pl.BlockSpec((tm, w), lambda i: (i, 0))
    consts = [wts["wg"], wts["pool_scale"], wts["wpp"], wts["wdp"], wts["wcp"], wts["wo"], wts["g_ffn"],
              wts["wr"], wts["br"]]
    return pl.pallas_call(
        _merge_kernel,
        grid=(n // tm,),
        in_specs=[row(d), row(d), row(d), row(N_BRANCH * d), row(d)] + [_const_spec(c.shape) for c in consts],
        out_specs=[row(d), pl.BlockSpec((tm * TOKEN_TILE_ROWS, LANES), lambda i: (i, 0)), row(EXPERT_LANES)],
        out_shape=[jax.ShapeDtypeStruct((n, d), F32), jax.ShapeDtypeStruct((n * TOKEN_TILE_ROWS, LANES), F32),
                   jax.ShapeDtypeStruct((n, EXPERT_LANES), F32)],
        compiler_params=_cparams(("parallel",), 48),
        name="merge",
    )(pooled, adiff, across, gl, x2d, *consts)


def _expert_kernel(be_ref, nv_ref, xs_ref, wu_ref, bu_ref, wd_ref, bd_ref, o_ref, wu_bf, wd_bf):
    i = pl.program_id(0)
    new_expert = jnp.logical_or(i == 0, be_ref[i] != be_ref[jnp.maximum(i - 1, 0)])

    @pl.when(jnp.logical_and(i < nv_ref[0], new_expert))
    def _cast_weights():
        wu_bf[...] = wu_ref[...].astype(BF16)
        wd_bf[...] = wd_ref[...].astype(BF16)

    @pl.when(i < nv_ref[0])
    def _run():
        x = jnp.concatenate(_load_token_tiles(xs_ref, MOE_ROWS), axis=1).astype(BF16)
        up = jnp.dot(x, wu_bf[...], preferred_element_type=F32) + bu_ref[...]
        glu = jnp.minimum(up[:, :D_FF], SWIGLU_LIMIT)
        lin = jnp.clip(up[:, D_FF:], -SWIGLU_LIMIT, SWIGLU_LIMIT)
        act = glu * jax.nn.sigmoid(SWIGLU_ALPHA * glu) * (lin + 1.0)
        _store_token_tiles(o_ref, jnp.dot(act.astype(BF16), wd_bf[...], preferred_element_type=F32) + bd_ref[...])

    @pl.when(i >= nv_ref[0])
    def _skip():
        o_ref[...] = jnp.zeros_like(o_ref)


def _experts(xs, block_e, n_valid, wu, bu, wd, bd):
    d = D_MODEL
    tile_rows = MOE_ROWS * TOKEN_TILE_ROWS
    grid_spec = pltpu.PrefetchScalarGridSpec(
        num_scalar_prefetch=2,
        grid=(xs.shape[0] // tile_rows,),
        in_specs=[pl.BlockSpec((tile_rows, LANES), lambda i, be, nv: (jnp.minimum(i, nv[0] - 1), 0)),
                  pl.BlockSpec((None, d, 2 * D_FF), lambda i, be, nv: (be[i], 0, 0)),
                  pl.BlockSpec((None, 1, 2 * D_FF), lambda i, be, nv: (be[i], 0, 0)),
                  pl.BlockSpec((None, D_FF, d), lambda i, be, nv: (be[i], 0, 0)),
                  pl.BlockSpec((None, 1, d), lambda i, be, nv: (be[i], 0, 0))],
        out_specs=pl.BlockSpec((tile_rows, LANES), lambda i, be, nv: (i, 0)),
        scratch_shapes=[pltpu.VMEM((d, 2 * D_FF), BF16), pltpu.VMEM((D_FF, d), BF16)])
    return pl.pallas_call(
        _expert_kernel,
        grid_spec=grid_spec,
        out_shape=jax.ShapeDtypeStruct(xs.shape, F32),
        compiler_params=_cparams(("arbitrary",), 56),
        name="experts",
    )(block_e, n_valid, xs, wu, bu, wd, bd)


def _slot_tile_copy(dest_hbm, dsm, sem, i, td):
    start = pl.multiple_of(i * (td * TOP_K), td * TOP_K)
    return pltpu.make_async_copy(dest_hbm.at[pl.ds(start, td * TOP_K)], dsm, sem)


def _token_rows(ref, idx, lead=()):
    start = pl.multiple_of(idx * TOKEN_TILE_ROWS, TOKEN_TILE_ROWS)
    return ref.at[lead + (pl.ds(start, TOKEN_TILE_ROWS), slice(None))]


def _combine_kernel(dest_hbm, ob_hbm, x1_ref, gate_ref, gf_ref, y_ref, dsm, rows, sem_d, sem_r, *, td):
    i = pl.program_id(0)
    slots = _slot_tile_copy(dest_hbm, dsm, sem_d, i, td)
    slots.start()
    slots.wait()

    def issue(r, carry):
        for k in range(TOP_K):
            pltpu.make_async_copy(_token_rows(ob_hbm, dsm[r * TOP_K + k]), _token_rows(rows, r, (k,)), sem_r).start()
        return carry

    lax.fori_loop(0, td, issue, 0, unroll=8)
    for k in range(TOP_K):
        pltpu.make_async_copy(ob_hbm.at[pl.ds(0, td * TOKEN_TILE_ROWS), :], rows.at[k], sem_r).wait()

    gate = gate_ref[...]
    parts = [_load_token_tiles(rows, td, (k,)) for k in range(TOP_K)]
    pieces = []
    for s in range(TOKEN_TILE_ROWS):
        moe = gate[:, 0:1] * parts[0][s]
        for k in range(1, TOP_K):
            moe = moe + gate[:, k:k + 1] * parts[k][s]
        pieces.append(x1_ref[:, s * LANES:(s + 1) * LANES] + moe)
    y_ref[...] = _rms(jnp.concatenate(pieces, axis=1), gf_ref[...])


def _combine(x1, out_buf, dest_flat, gate, g_final):
    n, d = x1.shape
    td = ROW_TILE
    return pl.pallas_call(
        functools.partial(_combine_kernel, td=td),
        grid=(n // td,),
        in_specs=[pl.BlockSpec(memory_space=pl.ANY), pl.BlockSpec(memory_space=pl.ANY),
                  pl.BlockSpec((td, d), lambda i: (i, 0)),
                  pl.BlockSpec((td, EXPERT_LANES), lambda i: (i, 0)),
                  _const_spec((1, d))],
        out_specs=pl.BlockSpec((td, d), lambda i: (i, 0)),
        out_shape=jax.ShapeDtypeStruct((n, d), F32),
        scratch_shapes=[pltpu.SMEM((td * TOP_K,), jnp.int32),
                        pltpu.VMEM((TOP_K, td * TOKEN_TILE_ROWS, LANES), F32),
                        pltpu.SemaphoreType.DMA, pltpu.SemaphoreType.DMA],
        compiler_params=_cparams(("arbitrary",), 32),
        name="combine_norm",
    )(dest_flat, out_buf, x1, gate, g_final.reshape(1, d))


def _dispatch_kernel(zrow_ref, dest_hbm, h2_ref, xs_hbm, dsm, zbuf, sem_d, sem_z, sem_r, *, td):
    i = pl.program_id(0)
    block_rows = MOE_ROWS * TOKEN_TILE_ROWS

    def zero_copy(e):
        return pltpu.make_async_copy(zbuf, xs_hbm.at[pl.ds(_tile_start(zrow_ref[e]), block_rows), :], sem_z)

    def tail_copy(blk):
        return pltpu.make_async_copy(zbuf, xs_hbm.at[pl.ds(_tile_start(blk * MOE_ROWS), block_rows), :], sem_z)

    @pl.when(i == 0)
    def _clear_unwritten_blocks():
        zbuf[...] = jnp.zeros_like(zbuf)
        first_unused = zrow_ref[N_EXPERTS]
        n_blocks = xs_hbm.shape[0] // block_rows
        for e in range(N_EXPERTS):
            @pl.when(zrow_ref[e] >= 0)
            def _start(e=e):
                zero_copy(e).start()
        lax.fori_loop(first_unused, n_blocks, lambda blk, c: (tail_copy(blk).start(), c)[1], 0)
        for e in range(N_EXPERTS):
            @pl.when(zrow_ref[e] >= 0)
            def _wait(e=e):
                zero_copy(e).wait()
        lax.fori_loop(first_unused, n_blocks, lambda blk, c: (tail_copy(blk).wait(), c)[1], 0)

    slots = _slot_tile_copy(dest_hbm, dsm, sem_d, i, td)
    slots.start()
    slots.wait()

    def issue(r, carry):
        src = _token_rows(h2_ref, r)
        for k in range(TOP_K):
            pltpu.make_async_copy(src, _token_rows(xs_hbm, dsm[r * TOP_K + k]), sem_r).start()
        return carry

    lax.fori_loop(0, td, issue, 0, unroll=8)
    for _ in range(TOP_K):
        pltpu.make_async_copy(h2_ref, xs_hbm.at[pl.ds(0, td * TOKEN_TILE_ROWS), :], sem_r).wait()


def _tile_start(slot):
    return pl.multiple_of(slot * TOKEN_TILE_ROWS, TOKEN_TILE_ROWS)


def _dispatch(h2_tiles, dest_flat, zero_rows, n_slots):
    td = ROW_TILE
    n = h2_tiles.shape[0] // TOKEN_TILE_ROWS
    grid_spec = pltpu.PrefetchScalarGridSpec(
        num_scalar_prefetch=1,
        grid=(n // td,),
        in_specs=[pl.BlockSpec(memory_space=pl.ANY),
                  pl.BlockSpec((td * TOKEN_TILE_ROWS, LANES), lambda i, zr: (i, 0))],
        out_specs=pl.BlockSpec(memory_space=pl.ANY),
        scratch_shapes=[pltpu.SMEM((td * TOP_K,), jnp.int32),
                        pltpu.VMEM((MOE_ROWS * TOKEN_TILE_ROWS, LANES), F32),
                        pltpu.SemaphoreType.DMA, pltpu.SemaphoreType.DMA, pltpu.SemaphoreType.DMA])
    return pl.pallas_call(
        functools.partial(_dispatch_kernel, td=td),
        grid_spec=grid_spec,
        out_shape=jax.ShapeDtypeStruct((n_slots * TOKEN_TILE_ROWS, LANES), F32),
        compiler_params=_cparams(("arbitrary",), 32),
        name="dispatch",
    )(zero_rows, dest_flat, h2_tiles)


def _route_kernel(lg_ref, eid_ref, gate_ref, rank_ref, cnt_ref, run_scr):
    i = pl.program_id(0)

    @pl.when(i == 0)
    def _init():
        run_scr[...] = jnp.zeros_like(run_scr)

    x = lg_ref[...]
    tm = x.shape[0]
    lane = lax.broadcasted_iota(jnp.int32, x.shape, 1).astype(F32)
    vals, hots = [], []
    eid = jnp.zeros(x.shape, F32)
    for k in range(TOP_K):
        m = jnp.max(x, axis=-1, keepdims=True)
        idx = jnp.min(jnp.where(x == m, lane, float(EXPERT_LANES)), axis=-1, keepdims=True)
        hot = lane == idx
        vals.append(m)
        hots.append(hot)
        eid = jnp.where(lane == float(k), idx, eid)
        x = jnp.where(hot, -jnp.inf, x)
    es = [jnp.exp(v - vals[0]) for v in vals]
    den = es[0]
    for e in es[1:]:
        den = den + e
    gate = jnp.zeros(x.shape, F32)
    for k in range(TOP_K):
        gate = jnp.where(lane == float(k), es[k] / den, gate)

    cnt_tok = hots[0].astype(F32)
    for hot in hots[1:]:
        cnt_tok = cnt_tok + hot.astype(F32)
    earlier = lax.broadcasted_iota(jnp.int32, (tm, tm), 1) < lax.broadcasted_iota(jnp.int32, (tm, tm), 0)
    before = run_scr[...] + jnp.dot(earlier.astype(BF16), cnt_tok.astype(BF16), preferred_element_type=F32)
    rank = jnp.zeros(x.shape, F32)
    for k in range(TOP_K):
        rank = jnp.where(lane == float(k), jnp.sum(jnp.where(hots[k], before, 0.0), axis=-1, keepdims=True), rank)
    run_scr[...] = run_scr[...] + jnp.sum(cnt_tok, axis=0, keepdims=True)

    eid_ref[...] = eid.astype(jnp.int32)
    gate_ref[...] = gate
    rank_ref[...] = rank.astype(jnp.int32)
    cnt_ref[...] = run_scr[...].astype(jnp.int32)


def _route(logits):
    n = logits.shape[0]
    tm = ROW_TILE
    row = pl.BlockSpec((tm, EXPERT_LANES), lambda i: (i, 0))
    eid, gate, rank, cnt = pl.pallas_call(
        _route_kernel,
        grid=(n // tm,),
        in_specs=[row],
        out_specs=[row, row, row, _const_spec((1, EXPERT_LANES))],
        out_shape=[jax.ShapeDtypeStruct((n, EXPERT_LANES), jnp.int32), jax.ShapeDtypeStruct((n, EXPERT_LANES), F32),
                   jax.ShapeDtypeStruct((n, EXPERT_LANES), jnp.int32),
                   jax.ShapeDtypeStruct((1, EXPERT_LANES), jnp.int32)],
        scratch_shapes=[pltpu.VMEM((1, EXPERT_LANES), F32)],
        compiler_params=_cparams(("arbitrary",), 32),
        name="route",
    )(logits)

    counts = cnt[0, :N_EXPERTS]
    padded = (counts + MOE_ROWS - 1) // MOE_ROWS * MOE_ROWS
    p_end = jnp.cumsum(padded)
    p_start = p_end - padded
    experts = jnp.arange(N_EXPERTS, dtype=jnp.int32)
    start_of = jnp.sum(jnp.where(eid[:, :TOP_K, None] == experts, p_start, 0), axis=-1)
    dest_flat = (start_of + rank[:, :TOP_K]).reshape(n * TOP_K).astype(jnp.int32)
    n_blocks = -(-(n * TOP_K + N_EXPERTS * (MOE_ROWS - 1)) // MOE_ROWS)
    block_e = jnp.minimum(jnp.searchsorted(p_end, jnp.arange(n_blocks, dtype=jnp.int32) * MOE_ROWS, side="right"),
                          N_EXPERTS - 1).astype(jnp.int32)
    n_valid = (p_end[-1] // MOE_ROWS).astype(jnp.int32).reshape(1)
    block_e = jnp.where(jnp.arange(n_blocks) < n_valid[0], block_e, block_e[n_valid[0] - 1])
    zero_rows = jnp.concatenate([jnp.where(padded > 0, p_end - MOE_ROWS, -1), n_valid]).astype(jnp.int32)
    return gate, dest_flat, block_e, n_valid, zero_rows, n_blocks * MOE_ROWS


def _in_splits(dt):
    off = np.cumsum([0, D_MODEL, Q_WIDTH, K_WIDTH, V_WIDTH, D_MODEL, N_BRANCH * D_MODEL])
    q_scale = HEAD_DIM ** -0.5 * math.log2(math.e)
    return ((int(off[0]), D_MODEL, 1.0, ((F32, PLAIN),)),
            (int(off[1]), Q_WIDTH, q_scale, ((dt, PLAIN),)),
            (int(off[2]), K_WIDTH, 1.0, ((F32, PLAIN), (BF16, PLAIN))),
            (int(off[3]), V_WIDTH, 1.0, ((F32, PLAIN), (BF16, WITH_ONES))),
            (int(off[4]), D_MODEL, X_DIM ** -0.5, ((dt, PLAIN),)),
            (int(off[5]), N_BRANCH * D_MODEL, 1.0, ((F32, PLAIN),)))


def kernel(x_prompt, x_sample, mem_prompt, cache_k, cache_v, cache_mem_k, cache_mem_v, state_pool, page_table, g_mix, w_in, w_pool_group, pool_scale, lambda_q1, lambda_k1, lambda_q2, lambda_k2, g_subln, g_mem, w_mem_kv, w_pool_proj, w_diff_proj, w_cross_proj, w_o, g_ffn, w_router, b_router, w_up, b_up, w_down, b_down, g_final):
    b, s, d = x_prompt.shape
    db, t, _ = x_sample.shape
    depth = w_in.shape[0]
    assert depth == 1, "the step is written for the single-layer trunk this problem states"
    n_phys, page = cache_k.shape[1], cache_k.shape[2]
    n_mem = mem_prompt.shape[1]
    n_p, n_s = b * s, db * t

    cache_kt = cache_k.transpose(0, 1, 3, 4, 5, 2).reshape(depth * n_phys * K_WIDTH, page)
    cache_v2 = cache_v.reshape(depth * n_phys * page * N_KV_HEADS, V_DIM)
    cmk = cache_mem_k.reshape(depth * db, n_mem, d)
    cmv = cache_mem_v.reshape(depth * db, n_mem, d)

    xp = x_prompt.reshape(n_p, d)
    xs = x_sample.reshape(n_s, d)
    outs = {name: [] for name in ("kp", "vp", "mk", "mv", "hp", "ks", "vs", "hs")}

    for l in range(depth):
        lam_init = 0.8 - 0.6 * math.exp(-0.3 * l)
        lam_vecs = jnp.stack([lambda_q1[l], lambda_k1[l], lambda_q2[l], lambda_k2[l]]).astype(F32)
        w_in_bf = w_in[l].astype(BF16)
        wr = jnp.zeros((d, EXPERT_LANES), BF16).at[:, :N_EXPERTS].set(w_router[l].astype(BF16))
        br = jnp.full((1, EXPERT_LANES), MASK_VALUE, F32).at[0, :N_EXPERTS].set(b_router[l].astype(F32))
        wts = dict(wg=w_pool_group[l].astype(BF16), pool_scale=pool_scale[l].reshape(1, d).astype(F32),
                   wpp=w_pool_proj[l].astype(BF16), wdp=w_diff_proj[l].astype(BF16),
                   wcp=w_cross_proj[l].astype(BF16), wo=w_o[l].astype(BF16),
                   g_ffn=g_ffn[l].reshape(1, d).astype(F32), wr=wr, br=br)

        mk, mv = _rms_proj(mem_prompt.reshape(b * n_mem, d), g_mem[l], w_mem_kv[l].astype(BF16),
                           ((0, d, 1.0, ((F32, PLAIN),)), (d, d, 1.0, ((F32, PLAIN),))))
        mk3, mv3 = mk.reshape(b, n_mem, d), mv.reshape(b, n_mem, d)

        u_p, q_p, k_p, k_pb, v_p, v_pb, qx_p, gl_p = _rms_proj(xp, g_mix[l], w_in_bf, _in_splits(BF16))
        pooled_p = _pool_prompt(u_p.reshape(b, s, d)).reshape(n_p, d)
        adiff_p = _attn_prompt(q_p.reshape(b, s, Q_WIDTH), k_pb.reshape(b, s, K_WIDTH),
                               v_pb.reshape(b, s, 2 * V_WIDTH), lam_vecs, g_subln[l], lam_init).reshape(n_p, d)
        across_p = _cross(qx_p.reshape(b, s, d), mk3, mv3, 0, BF16).reshape(n_p, d)
        x1_p, h2_p, lg_p = _merge(pooled_p, adiff_p, across_p, gl_p, xp, wts)

        u_s, q_s, k_s, _, v_s, _, qx_s, gl_s = _rms_proj(xs, g_mix[l], w_in_bf, _in_splits(F32))
        pooled_s = _pool_sample(state_pool[l], u_s.reshape(db, t, d)).reshape(n_s, d)
        adiff_s = _attn_sample(q_s.reshape(db, t, Q_WIDTH), k_s.reshape(db, t, K_WIDTH), v_s.reshape(db, t, V_WIDTH),
                               cache_kt, cache_v2, page, page_table + l * n_phys, lam_vecs, g_subln[l],
                               lam_init).reshape(n_s, d)
        across_s = _cross(qx_s.reshape(db, t, d), cmk, cmv, l * db, F32).reshape(n_s, d)
        x1_s, h2_s, lg_s = _merge(pooled_s, adiff_s, across_s, gl_s, xs, wts)

        x1 = jnp.concatenate([x1_p, x1_s], axis=0)
        h2 = jnp.concatenate([h2_p, h2_s], axis=0)
        gate, dest_flat, block_e, n_valid, zero_rows, n_slots = _route(jnp.concatenate([lg_p, lg_s], axis=0))
        out_buf = _experts(_dispatch(h2, dest_flat, zero_rows, n_slots), block_e, n_valid, w_up[l],
                           b_up[l].reshape(N_EXPERTS, 1, 2 * D_FF), w_down[l], b_down[l].reshape(N_EXPERTS, 1, d))
        y = _combine(x1, out_buf, dest_flat, gate, g_final)
        xp, xs = y[:n_p], y[n_p:]

        outs["kp"].append(k_p.reshape(b, s, N_KV_HEADS, 2, HEAD_DIM))
        outs["vp"].append(v_p.reshape(b, s, N_KV_HEADS, V_DIM))
        outs["mk"].append(mk3.reshape(b, n_mem, X_HEADS, X_DIM))
        outs["mv"].append(mv3.reshape(b, n_mem, X_HEADS, X_DIM))
        outs["hp"].append(u_p.reshape(b, s, d)[:, -POOL_STATE:].astype(state_pool.dtype))
        outs["ks"].append(k_s.reshape(db, t, N_KV_HEADS, 2, HEAD_DIM))
        outs["vs"].append(v_s.reshape(db, t, N_KV_HEADS, V_DIM))
        outs["hs"].append(jnp.concatenate([state_pool[l], u_s.reshape(db, t, d).astype(state_pool.dtype)],
                                          axis=1)[:, -POOL_STATE:])

    return (xp.reshape(b, s, d), xs.reshape(db, t, d), jnp.stack(outs["kp"]), jnp.stack(outs["vp"]),
            jnp.stack(outs["mk"]), jnp.stack(outs["mv"]), jnp.stack(outs["hp"]), jnp.stack(outs["ks"]),
            jnp.stack(outs["vs"]), jnp.stack(outs["hs"]))
```

```python
import functools
import math

import numpy as np
import jax
import jax.numpy as jnp
from jax import lax
from jax.experimental import pallas as pl
from jax.experimental.pallas import tpu as pltpu

F32 = jnp.float32
BF16 = jnp.bfloat16

D_MODEL = 1024
N_HEADS = 8
HEAD_DIM = 64
N_KV_HEADS = 4
GQA_GROUP = N_HEADS // N_KV_HEADS
V_DIM = 2 * HEAD_DIM
POOL_WINDOWS = (2, 4, 8, 16)
POOL_GROUPS = 4
POOL_GROUP_DIM = D_MODEL // POOL_GROUPS
POOL_STATE = 15
POOL_HALO = 16
X_HEADS = 4
X_DIM = D_MODEL // X_HEADS
N_BRANCH = 3
N_EXPERTS = 32
TOP_K = 4
D_FF = D_MODEL
SWIGLU_LIMIT = 7.0
SWIGLU_ALPHA = 1.702
RMS_EPS = 1e-6
Q_WIDTH = N_HEADS * 2 * HEAD_DIM
K_WIDTH = N_KV_HEADS * 2 * HEAD_DIM
V_WIDTH = N_KV_HEADS * V_DIM
KV_BLOCK = 2 * HEAD_DIM
MASK_VALUE = -1e30

LANES = 128
EXPERT_LANES = LANES
MIB = 2 ** 20

ROW_TILE = 256
ATTN_TQ = 512
ATTN_TK = 512
PAGES_PER_STEP = 16
CROSS_TQ = 512
MOE_ROWS = 512
DISPATCH_TILE = 1024
COMBINE_TILE = 512
POOL_SEQ_BLOCK = 8


def _cparams(semantics, vmem_mib):
    return pltpu.CompilerParams(dimension_semantics=semantics, vmem_limit_bytes=vmem_mib * MIB)


def _const_spec(shape):
    zeros = (0,) * len(shape)
    return pl.BlockSpec(shape, lambda *_: zeros)


def _rms(xf, g):
    return xf * lax.rsqrt(jnp.mean(xf * xf, axis=-1, keepdims=True) + RMS_EPS) * g


TOKEN_TILE_ROWS = D_MODEL // LANES


def _store_token_tiles(ref, x):
    n = x.shape[0]
    for s in range(TOKEN_TILE_ROWS):
        ref[pl.ds(s, n, stride=TOKEN_TILE_ROWS), :] = x[:, s * LANES:(s + 1) * LANES]


def _load_token_tiles(ref, n, lead=()):
    return [ref[lead + (pl.ds(s, n, stride=TOKEN_TILE_ROWS), slice(None))] for s in range(TOKEN_TILE_ROWS)]


PLAIN = ("plain",)
WITH_ONES = ("with_ones",)


def _chunk_rows(positions):
    return ("chunk_rows", tuple(positions))


VALUE_ROWS = _chunk_rows(range(N_KV_HEADS))
MEMORY_ROWS = _chunk_rows([(q % 2) * X_HEADS + q // 2 for q in range(X_HEADS * 2)])


def _rms_proj_kernel(x_ref, g_ref, w_ref, *rest, splits, transposed):
    hb = _rms(x_ref[...].astype(F32), g_ref[...]).astype(BF16)
    if transposed:
        wt_ref, rest = rest[0], rest[1:]
    out_refs = rest
    oi = 0
    for c0, width, scale, copies in splits:
        z = jnp.dot(hb, w_ref[:, c0:c0 + width], preferred_element_type=F32)
        if scale != 1.0:
            z = z * scale
        for dt, layout in copies:
            o_ref = out_refs[oi]
            oi += 1
            if layout == WITH_ONES:
                for h in range(width // V_DIM):
                    o_ref[:, 2 * h * V_DIM:(2 * h + 1) * V_DIM] = z[:, h * V_DIM:(h + 1) * V_DIM].astype(dt)
                    o_ref[:, (2 * h + 1) * V_DIM:(2 * h + 2) * V_DIM] = jnp.ones((z.shape[0], V_DIM), dt)
            elif layout[0] == "chunk_rows":
                positions = layout[1]
                for q, pos in enumerate(positions):
                    o_ref[pl.ds(pos, z.shape[0], stride=len(positions)), :] = z[:, q * LANES:(q + 1) * LANES].astype(dt)
            else:
                o_ref[...] = z.astype(dt)
    if transposed:
        zt = lax.dot_general(wt_ref[...], hb, (((1,), (1,)), ((), ())), preferred_element_type=F32)
        for dt in transposed:
            out_refs[oi][...] = zt.astype(dt)
            oi += 1


def _rms_proj(x2d, g, w_bf, splits, wt_bf=None, transposed=(), tokens_per_seq=None):
    n, d = x2d.shape
    tm = ROW_TILE
    assert n % tm == 0
    out_shape, out_specs = [], []
    for _, width, _, copies in splits:
        for dt, layout in copies:
            if layout[0] == "chunk_rows":
                n_chunks = len(layout[1])
                assert width == n_chunks * LANES
                shape, block = (n * n_chunks, LANES), (tm * n_chunks, LANES)
            else:
                w_out = 2 * width if layout == WITH_ONES else width
                shape, block = (n, w_out), (tm, w_out)
            out_shape.append(jax.ShapeDtypeStruct(shape, dt))
            out_specs.append(pl.BlockSpec(block, lambda i: (i, 0)))
    operands = [x2d, g.reshape(1, d), w_bf]
    in_specs = [pl.BlockSpec((tm, d), lambda i: (i, 0)), _const_spec((1, d)), _const_spec(w_bf.shape)]
    if transposed:
        assert tokens_per_seq % tm == 0 and n % tokens_per_seq == 0
        tiles_per_seq = tokens_per_seq // tm
        feats = wt_bf.shape[0]
        operands.append(wt_bf)
        in_specs.append(_const_spec(wt_bf.shape))
        for dt in transposed:
            out_shape.append(jax.ShapeDtypeStruct((n // tokens_per_seq, feats, tokens_per_seq), dt))
            out_specs.append(pl.BlockSpec((None, feats, tm), lambda i: (i // tiles_per_seq, 0, i % tiles_per_seq)))
    return pl.pallas_call(
        functools.partial(_rms_proj_kernel, splits=splits, transposed=tuple(transposed)),
        grid=(n // tm,),
        in_specs=in_specs,
        out_specs=out_specs,
        out_shape=out_shape,
        compiler_params=_cparams(("parallel",), 56),
        name="rms_proj",
    )(*operands)


def _pool_prompt_kernel(u_ref, halo_ref, o_ref, ue, *, tm):
    i = pl.program_id(1)
    ue[0:POOL_HALO, :] = jnp.where(i == 0, 0.0, halo_ref[...])
    ue[POOL_HALO:, :] = u_ref[...]
    pos = i * tm + lax.broadcasted_iota(jnp.int32, (tm, 1), 0)
    for g, w in enumerate(POOL_WINDOWS):
        c = slice(g * POOL_GROUP_DIM, (g + 1) * POOL_GROUP_DIM)
        tok = ue[POOL_HALO:POOL_HALO + tm, c]
        acc = tok
        for j in range(1, w):
            acc = acc + ue[POOL_HALO - j:POOL_HALO - j + tm, c]
        cnt = jnp.minimum(pos + 1, w).astype(F32)
        o_ref[:, c] = (acc / cnt - tok).astype(o_ref.dtype)


def _pool_prompt(u3):
    b, s, d = u3.shape
    tm = ROW_TILE
    halo_blocks = tm // POOL_HALO
    return pl.pallas_call(
        functools.partial(_pool_prompt_kernel, tm=tm),
        grid=(b, s // tm),
        in_specs=[pl.BlockSpec((None, tm, d), lambda bi, i: (bi, i, 0)),
                  pl.BlockSpec((None, POOL_HALO, d), lambda bi, i: (bi, jnp.maximum(i * halo_blocks - 1, 0), 0))],
        out_specs=pl.BlockSpec((None, tm, d), lambda bi, i: (bi, i, 0)),
        out_shape=jax.ShapeDtypeStruct((b, s, d), BF16),
        scratch_shapes=[pltpu.VMEM((tm + POOL_HALO, d), F32)],
        compiler_params=_cparams(("parallel", "parallel"), 32),
        name="pool_prompt",
    )(u3, u3)


def _pool_sample_kernel(ue_ref, o_ref, *, n_new):
    lo = ue_ref.shape[1] - n_new
    for s in range(ue_ref.shape[0]):
        for g, w in enumerate(POOL_WINDOWS):
            c = slice(g * POOL_GROUP_DIM, (g + 1) * POOL_GROUP_DIM)
            tok = ue_ref[s, lo:lo + n_new, c]
            acc = tok
            for j in range(1, w):
                acc = acc + ue_ref[s, lo - j:lo - j + n_new, c]
            o_ref[s, :, c] = acc / float(w) - tok


def _pool_sample(hist, u3):
    db, t, d = u3.shape
    assert hist.shape[1] >= POOL_WINDOWS[-1] - 1
    pad = (-(hist.shape[1] + t)) % 8
    ue = jnp.concatenate([jnp.zeros((db, pad, d), F32), hist.astype(F32), u3], axis=1)
    rows = ue.shape[1]
    sb = POOL_SEQ_BLOCK
    return pl.pallas_call(
        functools.partial(_pool_sample_kernel, n_new=t),
        grid=(db // sb,),
        in_specs=[pl.BlockSpec((sb, rows, d), lambda i: (i, 0, 0))],
        out_specs=pl.BlockSpec((sb, t, d), lambda i: (i, 0, 0)),
        out_shape=jax.ShapeDtypeStruct((db, t, d), F32),
        compiler_params=_cparams(("parallel",), 32),
        name="pool_sample",
    )(ue)


def _lambda_value(lam_ref, lam_init):
    a = jnp.sum(lam_ref[0:1, :] * lam_ref[1:2, :], axis=-1, keepdims=True)
    b = jnp.sum(lam_ref[2:3, :] * lam_ref[3:4, :], axis=-1, keepdims=True)
    return jnp.exp(a) - jnp.exp(b) + lam_init


def _subln(o, gsub, lam_init):
    return _rms(o, gsub) * (1.0 - lam_init)


def _softmax_update(s, m_prev):
    m_new = jnp.maximum(m_prev, jnp.max(s, axis=-1, keepdims=True))
    alpha = jnp.exp2(m_prev - m_new)
    p = jnp.exp2(s - jnp.tile(m_new, (1, s.shape[1] // LANES)))
    return p, m_new, alpha


def _attn_prompt_kernel(qi_ref, ki_ref, last_ref, q_ref, k_ref, v_ref, lam_ref, gsub_ref, o_ref,
                        qpad, acc, m_scr, *, tq, tk, lam_init):
    step = pl.program_id(1)
    qi = qi_ref[step]
    ki = ki_ref[step]
    n_streams = 2 * N_KV_HEADS

    @pl.when(ki == 0)
    def _init():
        lane = lax.broadcasted_iota(jnp.int32, (tq, KV_BLOCK), 1)
        for h in range(N_KV_HEADS):
            for g in range(GQA_GROUP):
                c0 = (h * GQA_GROUP + g) * KV_BLOCK
                blk = q_ref[:, c0:c0 + KV_BLOCK]
                rows = slice(g * tq, (g + 1) * tq)
                qpad[2 * h, rows, :] = jnp.where(lane < HEAD_DIM, blk, jnp.zeros_like(blk))
                qpad[2 * h + 1, rows, :] = jnp.where(lane >= HEAD_DIM, blk, jnp.zeros_like(blk))
        acc[...] = jnp.zeros_like(acc)
        m_scr[...] = jnp.full_like(m_scr, MASK_VALUE)

    items = [(idx, g) for idx in range(n_streams) for g in range(GQA_GROUP)]

    def scores(item):
        idx, g = item
        h = idx // 2
        return jnp.dot(qpad[idx, g * tq:(g + 1) * tq, :], k_ref[h * KV_BLOCK:(h + 1) * KV_BLOCK, :],
                       preferred_element_type=F32)

    def body(masked):
        if masked:
            qpos = qi * tq + lax.broadcasted_iota(jnp.int32, (tq, tk), 0)
            kpos = ki * tk + lax.broadcasted_iota(jnp.int32, (tq, tk), 1)
            vis = kpos <= qpos
        s_next = scores(items[0])
        for n, (idx, g) in enumerate(items):
            s = s_next
            if n + 1 < len(items):
                s_next = scores(items[n + 1])
            if masked:
                s = jnp.where(vis, s, MASK_VALUE)
            rows = slice(g * tq, (g + 1) * tq)
            p, m_new, alpha = _softmax_update(s, m_scr[idx, rows, :])
            m_scr[idx, rows, :] = m_new
            h = idx // 2
            pv = jnp.dot(p.astype(BF16), v_ref[:, h * 2 * V_DIM:(h + 1) * 2 * V_DIM], preferred_element_type=F32)
            acc[idx, rows, :] = jnp.tile(alpha, (1, 2)) * acc[idx, rows, :] + pv

    needs_mask = (ki + 1) * tk - 1 > qi * tq

    @pl.when(needs_mask)
    def _masked():
        body(True)

    @pl.when(jnp.logical_not(needs_mask))
    def _unmasked():
        body(False)

    @pl.when(last_ref[step] == 1)
    def _finish():
        lam = _lambda_value(lam_ref, lam_init)
        gsub = gsub_ref[...]
        for h in range(N_KV_HEADS):
            for g in range(GQA_GROUP):
                rows = slice(g * tq, (g + 1) * tq)
                o1 = acc[2 * h, rows, 0:V_DIM] / acc[2 * h, rows, V_DIM:2 * V_DIM]
                o2 = acc[2 * h + 1, rows, 0:V_DIM] / acc[2 * h + 1, rows, V_DIM:2 * V_DIM]
                c0 = (h * GQA_GROUP + g) * V_DIM
                o_ref[:, c0:c0 + V_DIM] = _subln(o1 - lam * o2, gsub, lam_init).astype(o_ref.dtype)


def _attn_prompt(q, k, v, lam_vecs, gsub, lam_init):
    b, s, _ = q.shape
    tq, tk = ATTN_TQ, ATTN_TK
    assert s % tq == 0 and s % tk == 0
    qi_l, ki_l, last_l = [], [], []
    for qi in range(s // tq):
        n_k = min(((qi + 1) * tq - 1) // tk + 1, s // tk)
        for ki in range(n_k):
            qi_l.append(qi)
            ki_l.append(ki)
            last_l.append(1 if ki == n_k - 1 else 0)
    tabs = [jnp.asarray(np.array(t, np.int32)) for t in (qi_l, ki_l, last_l)]
    rows = GQA_GROUP * tq
    grid_spec = pltpu.PrefetchScalarGridSpec(
        num_scalar_prefetch=3,
        grid=(b, len(qi_l)),
        in_specs=[pl.BlockSpec((None, tq, Q_WIDTH), lambda bi, st, qt, kt, lt: (bi, qt[st], 0)),
                  pl.BlockSpec((None, K_WIDTH, tk), lambda bi, st, qt, kt, lt: (bi, 0, kt[st])),
                  pl.BlockSpec((None, tk, 2 * V_WIDTH), lambda bi, st, qt, kt, lt: (bi, kt[st], 0)),
                  _const_spec((4, HEAD_DIM)), _const_spec((1, V_DIM))],
        out_specs=pl.BlockSpec((None, tq, N_HEADS * V_DIM), lambda bi, st, qt, kt, lt: (bi, qt[st], 0)),
        scratch_shapes=[pltpu.VMEM((2 * N_KV_HEADS, rows, KV_BLOCK), BF16),
                        pltpu.VMEM((2 * N_KV_HEADS, rows, 2 * V_DIM), F32),
                        pltpu.VMEM((2 * N_KV_HEADS, rows, LANES), F32)])
    return pl.pallas_call(
        functools.partial(_attn_prompt_kernel, tq=tq, tk=tk, lam_init=lam_init),
        grid_spec=grid_spec,
        out_shape=jax.ShapeDtypeStruct((b, s, N_HEADS * V_DIM), BF16),
        compiler_params=_cparams(("parallel", "arbitrary"), 56),
        name="attn_prompt",
    )(*tabs, q, k, v, lam_vecs, gsub.reshape(1, V_DIM))


SAMPLE_ROWS = N_KV_HEADS * 2 * GQA_GROUP


def _attn_sample_kernel(pt_ref, wq_ref, kn_ref, vn_ref, lam_ref, gsub_ref, kt_hbm, v_hbm, o_ref,
                        kbuf, vbuf, acc, m_scr, l_scr, sem_k, sem_v, *, n_pages, t_new, page, lam_init):
    seq = pl.program_id(0)
    step = pl.program_id(1)
    n_steps = pl.num_programs(1)
    rows = SAMPLE_ROWS * t_new
    k_rows = kbuf.shape[1] // n_pages
    v_rows = vbuf.shape[1] // n_pages
    flat = seq * n_steps + step
    slot = flat % 2

    def page_copies(seq_i, step_i, slot_i):
        copies = []
        for j in range(n_pages):
            pid = pt_ref[seq_i, step_i * n_pages + j]
            copies.append(pltpu.make_async_copy(kt_hbm.at[pl.ds(pl.multiple_of(pid * k_rows, k_rows), k_rows), :],
                                                kbuf.at[slot_i, pl.ds(j * k_rows, k_rows), :], sem_k.at[slot_i]))
            copies.append(pltpu.make_async_copy(v_hbm.at[pl.ds(pl.multiple_of(pid * v_rows, v_rows), v_rows), :],
                                                vbuf.at[slot_i, pl.ds(j * v_rows, v_rows), :], sem_v.at[slot_i]))
        return copies

    @pl.when(flat == 0)
    def _first_fetch():
        for c in page_copies(seq, step, slot):
            c.start()

    @pl.when(flat + 1 < pl.num_programs(0) * n_steps)
    def _prefetch():
        wrap = step + 1 == n_steps
        for c in page_copies(jnp.where(wrap, seq + 1, seq), jnp.where(wrap, 0, step + 1), 1 - slot):
            c.start()

    @pl.when(step == 0)
    def _init():
        acc[...] = jnp.zeros_like(acc)
        m_scr[...] = jnp.full_like(m_scr, MASK_VALUE)
        l_scr[...] = jnp.zeros_like(l_scr)

    pltpu.make_async_copy(kt_hbm.at[pl.ds(0, n_pages * k_rows), :], kbuf.at[slot], sem_k.at[slot]).wait()
    pltpu.make_async_copy(v_hbm.at[pl.ds(0, n_pages * v_rows), :], vbuf.at[slot], sem_v.at[slot]).wait()

    wq = wq_ref[...]

    def update(s, v_list):
        p, m_new, alpha = _softmax_update(s, m_scr[...])
        m_scr[...] = m_new
        l_scr[...] = alpha * l_scr[...] + jnp.sum(p, axis=-1, keepdims=True)
        pb = p.astype(BF16)
        pv = None
        for j, vj in enumerate(v_list):
            c = vj.shape[0]
            term = jnp.dot(pb[:, j * c:(j + 1) * c], vj, preferred_element_type=F32)
            pv = term if pv is None else pv + term
        acc[...] = jnp.tile(alpha, (1, N_KV_HEADS)) * acc[...] + pv

    def page_keys(j):
        return kbuf[slot, pl.ds(j * k_rows, k_rows), :].astype(BF16)

    def page_values(j):
        heads = [vbuf[slot, pl.ds(j * v_rows + h, page, stride=N_KV_HEADS), :] for h in range(N_KV_HEADS)]
        return jnp.concatenate(heads, axis=1).astype(BF16)

    s_list = [jnp.dot(wq, page_keys(j), preferred_element_type=F32) for j in range(n_pages)]
    update(jnp.concatenate(s_list, axis=1), [page_values(j) for j in range(n_pages)])

    @pl.when(step == pl.num_programs(1) - 1)
    def _finish():
        pad = jnp.zeros((LANES - t_new, kn_ref.shape[1]), F32)
        kn = jnp.concatenate([kn_ref[...], pad], axis=0).astype(BF16)
        vn = jnp.concatenate([vn_ref[...], pad], axis=0).astype(BF16)
        s_new = lax.dot_general(wq, kn, (((1,), (1,)), ((), ())), preferred_element_type=F32)
        t_of_row = lax.broadcasted_iota(jnp.int32, (rows, LANES), 0) % t_new
        key = lax.broadcasted_iota(jnp.int32, (rows, LANES), 1)
        update(jnp.where(key <= t_of_row, s_new, MASK_VALUE), [vn])

        lam = _lambda_value(lam_ref, lam_init)
        gsub = gsub_ref[...]
        for h in range(N_KV_HEADS):
            for g in range(GQA_GROUP):
                r1 = (h * 2 * GQA_GROUP + g) * t_new
                r2 = (h * 2 * GQA_GROUP + GQA_GROUP + g) * t_new
                cols = slice(h * V_DIM, (h + 1) * V_DIM)
                o1 = acc[r1:r1 + t_new, cols] / l_scr[r1:r1 + t_new, :]
                o2 = acc[r2:r2 + t_new, cols] / l_scr[r2:r2 + t_new, :]
                c0 = (h * GQA_GROUP + g) * V_DIM
                o_ref[:, c0:c0 + V_DIM] = _subln(o1 - lam * o2, gsub, lam_init).astype(o_ref.dtype)


def _sample_query_rows(q3):
    db, t, _ = q3.shape
    q6 = q3.reshape(db, t, N_KV_HEADS, GQA_GROUP, 2, HEAD_DIM).transpose(0, 2, 4, 3, 1, 5)
    eye = jnp.eye(N_KV_HEADS * 2, dtype=q3.dtype).reshape(N_KV_HEADS, 2, N_KV_HEADS, 2)
    wq = jnp.einsum("bhmgtd,hmHM->bhmgtHMd", q6, eye)
    return wq.reshape(db, SAMPLE_ROWS * t, K_WIDTH).astype(BF16)


def _attn_sample(q3, k_new, v_new, cache_kt, cache_v, page, page_ids, lam_vecs, gsub, lam_init):
    db, t, _ = q3.shape
    n_pages_total = page_ids.shape[1]
    g = PAGES_PER_STEP
    rows = SAMPLE_ROWS * t
    assert n_pages_total % g == 0 and rows % 8 == 0 and t <= LANES and page % LANES == 0
    wq = _sample_query_rows(q3)

    assert page == LANES and cache_kt.shape[1] == page and cache_v.shape[1] == V_DIM
    grid_spec = pltpu.PrefetchScalarGridSpec(
        num_scalar_prefetch=1,
        grid=(db, n_pages_total // g),
        in_specs=[pl.BlockSpec((None, rows, K_WIDTH), lambda bi, st, pt: (bi, 0, 0)),
                  pl.BlockSpec((None, t, K_WIDTH), lambda bi, st, pt: (bi, 0, 0)),
                  pl.BlockSpec((None, t, V_WIDTH), lambda bi, st, pt: (bi, 0, 0)),
                  _const_spec((4, HEAD_DIM)), _const_spec((1, V_DIM)),
                  pl.BlockSpec(memory_space=pl.ANY), pl.BlockSpec(memory_space=pl.ANY)],
        out_specs=pl.BlockSpec((None, t, N_HEADS * V_DIM), lambda bi, st, pt: (bi, 0, 0)),
        scratch_shapes=[pltpu.VMEM((2, g * K_WIDTH, page), F32),
                        pltpu.VMEM((2, g * page * N_KV_HEADS, V_DIM), F32),
                        pltpu.VMEM((rows, V_WIDTH), F32),
                        pltpu.VMEM((rows, LANES), F32),
                        pltpu.VMEM((rows, LANES), F32),
                        pltpu.SemaphoreType.DMA((2,)), pltpu.SemaphoreType.DMA((2,))])
    return pl.pallas_call(
        functools.partial(_attn_sample_kernel, n_pages=g, t_new=t, page=page, lam_init=lam_init),
        grid_spec=grid_spec,
        out_shape=jax.ShapeDtypeStruct((db, t, N_HEADS * V_DIM), F32),
        compiler_params=_cparams(("arbitrary", "arbitrary"), 40),
        name="attn_sample",
    )(page_ids, wq, k_new, v_new, lam_vecs, gsub.reshape(1, V_DIM), cache_kt, cache_v)


def _cross_kernel(q_ref, mk_ref, mv_ref, o_ref, *, n_mem):
    halves = X_DIM // LANES
    rows_per_token = X_HEADS * halves

    def head(ref, h):
        parts = [ref[pl.ds(c * X_HEADS + h, n_mem, stride=rows_per_token), :] for c in range(halves)]
        return jnp.concatenate(parts, axis=1).astype(BF16)

    for h in range(X_HEADS):
        c = slice(h * X_DIM, (h + 1) * X_DIM)
        q = q_ref[:, c].astype(BF16)
        s = lax.dot_general(q, head(mk_ref, h), (((1,), (1,)), ((), ())), preferred_element_type=F32)
        p = jnp.exp(s - jnp.max(s, axis=-1, keepdims=True))
        l = jnp.sum(p, axis=-1, keepdims=True)
        o = jnp.dot(p.astype(BF16), head(mv_ref, h), preferred_element_type=F32)
        o_ref[:, c] = (o / l).astype(o_ref.dtype)


def _memory_rows(mem5):
    l, b, n_mem, heads, xd = mem5.shape
    m6 = mem5.reshape(l, b, n_mem, heads, xd // LANES, LANES).transpose(0, 1, 2, 4, 3, 5)
    return m6.reshape(l * b * n_mem * heads * (xd // LANES), LANES)


def _memory_from_rows(rows, l, b, n_mem):
    halves = X_DIM // LANES
    return rows.reshape(l, b, n_mem, halves, X_HEADS, LANES).transpose(0, 1, 2, 4, 3, 5).reshape(
        l, b, n_mem, X_HEADS, X_DIM)


def _cross(q3, mk, mv, n_mem, mem_offset, out_dtype):
    b, t, d = q3.shape
    tq = min(CROSS_TQ, t)
    mem_rows = n_mem * (d // LANES)
    return pl.pallas_call(
        functools.partial(_cross_kernel, n_mem=n_mem),
        grid=(b, t // tq),
        in_specs=[pl.BlockSpec((None, tq, d), lambda bi, i: (bi, i, 0)),
                  pl.BlockSpec((mem_rows, LANES), lambda bi, i: (bi + mem_offset, 0)),
                  pl.BlockSpec((mem_rows, LANES), lambda bi, i: (bi + mem_offset, 0))],
        out_specs=pl.BlockSpec((None, tq, d), lambda bi, i: (bi, i, 0)),
        out_shape=jax.ShapeDtypeStruct((b, t, d), out_dtype),
        compiler_params=_cparams(("parallel", "parallel"), 32),
        name="cross_attn",
    )(q3, mk, mv)


def _merge_kernel(pooled_ref, adiff_ref, across_ref, gl_ref, x_ref, wg_ref, ps_ref, wpp_ref, wdp_ref, wcp_ref,
                  wo_ref, gffn_ref, wr_ref, br_ref, x1_ref, h2_ref, lg_ref):
    pooled = pooled_ref[...].astype(BF16)
    ys = [jnp.dot(pooled[:, g * POOL_GROUP_DIM:(g + 1) * POOL_GROUP_DIM], wg_ref[g], preferred_element_type=F32)
          for g in range(POOL_GROUPS)]
    a_pool = jnp.concatenate(ys, axis=1) * ps_ref[...]
    p_pool = jnp.dot(a_pool.astype(BF16), wpp_ref[...], preferred_element_type=F32)
    p_diff = jnp.dot(adiff_ref[...].astype(BF16), wdp_ref[...], preferred_element_type=F32)
    p_cross = jnp.dot(across_ref[...].astype(BF16), wcp_ref[...], preferred_element_type=F32)
    merged = (jax.nn.sigmoid(gl_ref[:, 0:D_MODEL]) * p_pool
              + jax.nn.sigmoid(gl_ref[:, D_MODEL:2 * D_MODEL]) * p_diff
              + jax.nn.sigmoid(gl_ref[:, 2 * D_MODEL:3 * D_MODEL]) * p_cross)
    x1 = x_ref[...] + jnp.dot(merged.astype(BF16), wo_ref[...], preferred_element_type=F32)
    x1_ref[...] = x1
    h2 = _rms(x1, gffn_ref[...])
    _store_token_tiles(h2_ref, h2)
    lg_ref[...] = jnp.dot(h2.astype(BF16), wr_ref[...], preferred_element_type=F32) + br_ref[...]


def _merge(pooled, adiff, across, gl, x2d, wts):
    n, d = x2d.shape
    tm = ROW_TILE
    row = lambda w: pl.BlockSpec((tm, w), lambda i: (i, 0))
    consts = [wts["wg"], wts["pool_scale"], wts["wpp"], wts["wdp"], wts["wcp"], wts["wo"], wts["g_ffn"],
              wts["wr"], wts["br"]]
    return pl.pallas_call(
        _merge_kernel,
        grid=(n // tm,),
        in_specs=[row(d), row(d), row(d), row(N_BRANCH * d), row(d)] + [_const_spec(c.shape) for c in consts],
        out_specs=[row(d), pl.BlockSpec((tm * TOKEN_TILE_ROWS, LANES), lambda i: (i, 0)), row(EXPERT_LANES)],
        out_shape=[jax.ShapeDtypeStruct((n, d), F32), jax.ShapeDtypeStruct((n * TOKEN_TILE_ROWS, LANES), F32),
                   jax.ShapeDtypeStruct((n, EXPERT_LANES), F32)],
        compiler_params=_cparams(("parallel",), 48),
        name="merge",
    )(pooled, adiff, across, gl, x2d, *consts)


def _expert_kernel(be_ref, nv_ref, xs_ref, wu_ref, bu_ref, wd_ref, bd_ref, o_ref, wu_bf, wd_bf):
    i = pl.program_id(0)
    new_expert = jnp.logical_or(i == 0, be_ref[i] != be_ref[jnp.maximum(i - 1, 0)])

    @pl.when(jnp.logical_and(i < nv_ref[0], new_expert))
    def _cast_weights():
        wu_bf[...] = wu_ref[...].astype(BF16)
        wd_bf[...] = wd_ref[...].astype(BF16)

    @pl.when(i < nv_ref[0])
    def _run():
        x = jnp.concatenate(_load_token_tiles(xs_ref, MOE_ROWS), axis=1).astype(BF16)
        up = jnp.dot(x, wu_bf[...], preferred_element_type=F32) + bu_ref[...]
        glu = jnp.minimum(up[:, :D_FF], SWIGLU_LIMIT)
        lin = jnp.clip(up[:, D_FF:], -SWIGLU_LIMIT, SWIGLU_LIMIT)
        act = glu * jax.nn.sigmoid(SWIGLU_ALPHA * glu) * (lin + 1.0)
        _store_token_tiles(o_ref, jnp.dot(act.astype(BF16), wd_bf[...], preferred_element_type=F32) + bd_ref[...])

    @pl.when(i >= nv_ref[0])
    def _skip():
        o_ref[...] = jnp.zeros_like(o_ref)


def _experts(xs, block_e, n_valid, wu, bu, wd, bd):
    d = D_MODEL
    tile_rows = MOE_ROWS * TOKEN_TILE_ROWS
    grid_spec = pltpu.PrefetchScalarGridSpec(
        num_scalar_prefetch=2,
        grid=(xs.shape[0] // tile_rows,),
        in_specs=[pl.BlockSpec((tile_rows, LANES), lambda i, be, nv: (jnp.minimum(i, nv[0] - 1), 0)),
                  pl.BlockSpec((None, d, 2 * D_FF), lambda i, be, nv: (be[i], 0, 0)),
                  pl.BlockSpec((None, 1, 2 * D_FF), lambda i, be, nv: (be[i], 0, 0)),
                  pl.BlockSpec((None, D_FF, d), lambda i, be, nv: (be[i], 0, 0)),
                  pl.BlockSpec((None, 1, d), lambda i, be, nv: (be[i], 0, 0))],
        out_specs=pl.BlockSpec((tile_rows, LANES), lambda i, be, nv: (i, 0)),
        scratch_shapes=[pltpu.VMEM((d, 2 * D_FF), BF16), pltpu.VMEM((D_FF, d), BF16)])
    return pl.pallas_call(
        _expert_kernel,
        grid_spec=grid_spec,
        out_shape=jax.ShapeDtypeStruct(xs.shape, F32),
        compiler_params=_cparams(("arbitrary",), 56),
        name="experts",
    )(block_e, n_valid, xs, wu, bu, wd, bd)


def _slot_tile_copy(dest_hbm, dsm, sem, i, td):
    start = pl.multiple_of(i * (td * TOP_K), td * TOP_K)
    return pltpu.make_async_copy(dest_hbm.at[pl.ds(start, td * TOP_K)], dsm, sem)


def _token_rows(ref, idx, lead=()):
    start = pl.multiple_of(idx * TOKEN_TILE_ROWS, TOKEN_TILE_ROWS)
    return ref.at[lead + (pl.ds(start, TOKEN_TILE_ROWS), slice(None))]


def _combine_kernel(dest_hbm, ob_hbm, x1_ref, gate_ref, gf_ref, y_ref, dsm, rows, sem_d, sem_r, *, td):
    i = pl.program_id(0)
    slots = _slot_tile_copy(dest_hbm, dsm, sem_d, i, td)
    slots.start()
    slots.wait()

    def issue(r, carry):
        for k in range(TOP_K):
            pltpu.make_async_copy(_token_rows(ob_hbm, dsm[r * TOP_K + k]), _token_rows(rows, r, (k,)), sem_r).start()
        return carry

    lax.fori_loop(0, td, issue, 0, unroll=8)
    for k in range(TOP_K):
        pltpu.make_async_copy(ob_hbm.at[pl.ds(0, td * TOKEN_TILE_ROWS), :], rows.at[k], sem_r).wait()

    gate = gate_ref[...]
    parts = [_load_token_tiles(rows, td, (k,)) for k in range(TOP_K)]
    pieces = []
    for s in range(TOKEN_TILE_ROWS):
        moe = gate[:, 0:1] * parts[0][s]
        for k in range(1, TOP_K):
            moe = moe + gate[:, k:k + 1] * parts[k][s]
        pieces.append(x1_ref[:, s * LANES:(s + 1) * LANES] + moe)
    y_ref[...] = _rms(jnp.concatenate(pieces, axis=1), gf_ref[...])


def _combine(x1, out_buf, dest_flat, gate, g_final):
    n, d = x1.shape
    td = COMBINE_TILE
    assert n % td == 0
    return pl.pallas_call(
        functools.partial(_combine_kernel, td=td),
        grid=(n // td,),
        in_specs=[pl.BlockSpec(memory_space=pl.ANY), pl.BlockSpec(memory_space=pl.ANY),
                  pl.BlockSpec((td, d), lambda i: (i, 0)),
                  pl.BlockSpec((td, EXPERT_LANES), lambda i: (i, 0)),
                  _const_spec((1, d))],
        out_specs=pl.BlockSpec((td, d), lambda i: (i, 0)),
        out_shape=jax.ShapeDtypeStruct((n, d), F32),
        scratch_shapes=[pltpu.SMEM((td * TOP_K,), jnp.int32),
                        pltpu.VMEM((TOP_K, td * TOKEN_TILE_ROWS, LANES), F32),
                        pltpu.SemaphoreType.DMA, pltpu.SemaphoreType.DMA],
        compiler_params=_cparams(("arbitrary",), 32),
        name="combine_norm",
    )(dest_flat, out_buf, x1, gate, g_final.reshape(1, d))


def _dispatch_kernel(zrow_ref, dest_hbm, h2_ref, xs_hbm, dsm, zbuf, sem_d, sem_z, sem_r, *, td):
    i = pl.program_id(0)
    block_rows = MOE_ROWS * TOKEN_TILE_ROWS

    def zero_copy(e):
        return pltpu.make_async_copy(zbuf, xs_hbm.at[pl.ds(_tile_start(zrow_ref[e]), block_rows), :], sem_z)

    def tail_copy(blk):
        return pltpu.make_async_copy(zbuf, xs_hbm.at[pl.ds(_tile_start(blk * MOE_ROWS), block_rows), :], sem_z)

    @pl.when(i == 0)
    def _clear_unwritten_blocks():
        zbuf[...] = jnp.zeros_like(zbuf)
        first_unused = zrow_ref[N_EXPERTS]
        n_blocks = xs_hbm.shape[0] // block_rows
        for e in range(N_EXPERTS):
            @pl.when(zrow_ref[e] >= 0)
            def _start(e=e):
                zero_copy(e).start()
        lax.fori_loop(first_unused, n_blocks, lambda blk, c: (tail_copy(blk).start(), c)[1], 0)
        for e in range(N_EXPERTS):
            @pl.when(zrow_ref[e] >= 0)
            def _wait(e=e):
                zero_copy(e).wait()
        lax.fori_loop(first_unused, n_blocks, lambda blk, c: (tail_copy(blk).wait(), c)[1], 0)

    slots = _slot_tile_copy(dest_hbm, dsm, sem_d, i, td)
    slots.start()
    slots.wait()

    def issue(r, carry):
        src = _token_rows(h2_ref, r)
        for k in range(TOP_K):
            pltpu.make_async_copy(src, _token_rows(xs_hbm, dsm[r * TOP_K + k]), sem_r).start()
        return carry

    lax.fori_loop(0, td, issue, 0, unroll=8)
    for _ in range(TOP_K):
        pltpu.make_async_copy(h2_ref, xs_hbm.at[pl.ds(0, td * TOKEN_TILE_ROWS), :], sem_r).wait()


def _tile_start(slot):
    return pl.multiple_of(slot * TOKEN_TILE_ROWS, TOKEN_TILE_ROWS)


def _dispatch(h2_tiles, dest_flat, zero_rows, n_slots):
    td = DISPATCH_TILE
    n = h2_tiles.shape[0] // TOKEN_TILE_ROWS
    assert n % td == 0
    grid_spec = pltpu.PrefetchScalarGridSpec(
        num_scalar_prefetch=1,
        grid=(n // td,),
        in_specs=[pl.BlockSpec(memory_space=pl.ANY),
                  pl.BlockSpec((td * TOKEN_TILE_ROWS, LANES), lambda i, zr: (i, 0))],
        out_specs=pl.BlockSpec(memory_space=pl.ANY),
        scratch_shapes=[pltpu.SMEM((td * TOP_K,), jnp.int32),
                        pltpu.VMEM((MOE_ROWS * TOKEN_TILE_ROWS, LANES), F32),
                        pltpu.SemaphoreType.DMA, pltpu.SemaphoreType.DMA, pltpu.SemaphoreType.DMA])
    return pl.pallas_call(
        functools.partial(_dispatch_kernel, td=td),
        grid_spec=grid_spec,
        out_shape=jax.ShapeDtypeStruct((n_slots * TOKEN_TILE_ROWS, LANES), F32),
        compiler_params=_cparams(("arbitrary",), 32),
        name="dispatch",
    )(zero_rows, dest_flat, h2_tiles)


def _route_kernel(lg_ref, eid_ref, gate_ref, rank_ref, cnt_ref, run_scr):
    i = pl.program_id(0)

    @pl.when(i == 0)
    def _init():
        run_scr[...] = jnp.zeros_like(run_scr)

    x = lg_ref[...]
    tm = x.shape[0]
    lane = lax.broadcasted_iota(jnp.int32, x.shape, 1).astype(F32)
    vals, hots = [], []
    eid = jnp.zeros(x.shape, F32)
    for k in range(TOP_K):
        m = jnp.max(x, axis=-1, keepdims=True)
        idx = jnp.min(jnp.where(x == m, lane, float(EXPERT_LANES)), axis=-1, keepdims=True)
        hot = lane == idx
        vals.append(m)
        hots.append(hot)
        eid = jnp.where(lane == float(k), idx, eid)
        x = jnp.where(hot, -jnp.inf, x)
    es = [jnp.exp(v - vals[0]) for v in vals]
    den = es[0]
    for e in es[1:]:
        den = den + e
    gate = jnp.zeros(x.shape, F32)
    for k in range(TOP_K):
        gate = jnp.where(lane == float(k), es[k] / den, gate)

    cnt_tok = hots[0].astype(F32)
    for hot in hots[1:]:
        cnt_tok = cnt_tok + hot.astype(F32)
    earlier = lax.broadcasted_iota(jnp.int32, (tm, tm), 1) < lax.broadcasted_iota(jnp.int32, (tm, tm), 0)
    before = run_scr[...] + jnp.dot(earlier.astype(BF16), cnt_tok.astype(BF16), preferred_element_type=F32)
    rank = jnp.zeros(x.shape, F32)
    for k in range(TOP_K):
        rank = jnp.where(lane == float(k), jnp.sum(jnp.where(hots[k], before, 0.0), axis=-1, keepdims=True), rank)
    run_scr[...] = run_scr[...] + jnp.sum(cnt_tok, axis=0, keepdims=True)

    eid_ref[...] = eid.astype(jnp.int32)
    gate_ref[...] = gate
    rank_ref[...] = rank.astype(jnp.int32)
    cnt_ref[...] = run_scr[...].astype(jnp.int32)


def _route(logits):
    n = logits.shape[0]
    tm = ROW_TILE
    row = pl.BlockSpec((tm, EXPERT_LANES), lambda i: (i, 0))
    eid, gate, rank, cnt = pl.pallas_call(
        _route_kernel,
        grid=(n // tm,),
        in_specs=[row],
        out_specs=[row, row, row, _const_spec((1, EXPERT_LANES))],
        out_shape=[jax.ShapeDtypeStruct((n, EXPERT_LANES), jnp.int32), jax.ShapeDtypeStruct((n, EXPERT_LANES), F32),
                   jax.ShapeDtypeStruct((n, EXPERT_LANES), jnp.int32),
                   jax.ShapeDtypeStruct((1, EXPERT_LANES), jnp.int32)],
        scratch_shapes=[pltpu.VMEM((1, EXPERT_LANES), F32)],
        compiler_params=_cparams(("arbitrary",), 32),
        name="route",
    )(logits)

    counts = cnt[0, :N_EXPERTS]
    padded = (counts + MOE_ROWS - 1) // MOE_ROWS * MOE_ROWS
    p_end = jnp.cumsum(padded)
    p_start = p_end - padded
    experts = jnp.arange(N_EXPERTS, dtype=jnp.int32)
    start_of = jnp.sum(jnp.where(eid[:, :TOP_K, None] == experts, p_start, 0), axis=-1)
    dest_flat = (start_of + rank[:, :TOP_K]).reshape(n * TOP_K).astype(jnp.int32)
    n_blocks = -(-(n * TOP_K + N_EXPERTS * (MOE_ROWS - 1)) // MOE_ROWS)
    block_start = jnp.arange(n_blocks, dtype=jnp.int32) * MOE_ROWS
    block_e = jnp.minimum(jnp.sum(p_end[None, :] <= block_start[:, None], axis=1), N_EXPERTS - 1).astype(jnp.int32)
    n_valid = (p_end[-1] // MOE_ROWS).astype(jnp.int32).reshape(1)
    block_e = jnp.where(jnp.arange(n_blocks) < n_valid[0], block_e, block_e[n_valid[0] - 1])
    zero_rows = jnp.concatenate([jnp.where(padded > 0, p_end - MOE_ROWS, -1), n_valid]).astype(jnp.int32)
    return gate, dest_flat, block_e, n_valid, zero_rows, n_blocks * MOE_ROWS


def _in_splits(dt, v_layout, with_keys):
    off = np.cumsum([0, D_MODEL, Q_WIDTH, K_WIDTH, V_WIDTH, D_MODEL, N_BRANCH * D_MODEL])
    q_scale = HEAD_DIM ** -0.5 * math.log2(math.e)
    keys = ((int(off[2]), K_WIDTH, 1.0, ((F32, PLAIN),)),) if with_keys else ()
    return ((int(off[0]), D_MODEL, 1.0, ((F32, PLAIN),)),
            (int(off[1]), Q_WIDTH, q_scale, ((dt, PLAIN),)),
            *keys,
            (int(off[3]), V_WIDTH, 1.0, ((F32, v_layout), (BF16, WITH_ONES))),
            (int(off[4]), D_MODEL, X_DIM ** -0.5, ((dt, PLAIN),)),
            (int(off[5]), N_BRANCH * D_MODEL, 1.0, ((F32, PLAIN),)))


def kernel(x_prompt, x_sample, mem_prompt, cache_k, cache_v, cache_mem_k, cache_mem_v, state_pool, page_table, g_mix, w_in, w_pool_group, pool_scale, lambda_q1, lambda_k1, lambda_q2, lambda_k2, g_subln, g_mem, w_mem_kv, w_pool_proj, w_diff_proj, w_cross_proj, w_o, g_ffn, w_router, b_router, w_up, b_up, w_down, b_down, g_final):
    b, s, d = x_prompt.shape
    db, t, _ = x_sample.shape
    depth = w_in.shape[0]
    assert depth == 1, "the step is written for the single-layer trunk this problem states"
    n_phys, page = cache_k.shape[1], cache_k.shape[2]
    n_mem = mem_prompt.shape[1]
    n_p, n_s = b * s, db * t

    cache_kt = cache_k.transpose(0, 1, 3, 4, 5, 2).reshape(depth * n_phys * K_WIDTH, page)
    cache_v2 = cache_v.reshape(depth * n_phys * page * N_KV_HEADS, V_DIM)
    cmk = _memory_rows(cache_mem_k)
    cmv = _memory_rows(cache_mem_v)

    xp = x_prompt.reshape(n_p, d)
    xs = x_sample.reshape(n_s, d)
    outs = {name: [] for name in ("kp", "vp", "mk", "mv", "hp", "ks", "vs", "hs")}

    for l in range(depth):
        lam_init = 0.8 - 0.6 * math.exp(-0.3 * l)
        lam_vecs = jnp.stack([lambda_q1[l], lambda_k1[l], lambda_q2[l], lambda_k2[l]]).astype(F32)
        w_in_bf = w_in[l].astype(BF16)
        wr = jnp.zeros((d, EXPERT_LANES), BF16).at[:, :N_EXPERTS].set(w_router[l].astype(BF16))
        br = jnp.full((1, EXPERT_LANES), MASK_VALUE, F32).at[0, :N_EXPERTS].set(b_router[l].astype(F32))
        wts = dict(wg=w_pool_group[l].astype(BF16), pool_scale=pool_scale[l].reshape(1, d).astype(F32),
                   wpp=w_pool_proj[l].astype(BF16), wdp=w_diff_proj[l].astype(BF16),
                   wcp=w_cross_proj[l].astype(BF16), wo=w_o[l].astype(BF16),
                   g_ffn=g_ffn[l].reshape(1, d).astype(F32), wr=wr, br=br)

        mk, mv = _rms_proj(mem_prompt.reshape(b * n_mem, d), g_mem[l], w_mem_kv[l].astype(BF16),
                           ((0, d, 1.0, ((F32, MEMORY_ROWS),)), (d, d, 1.0, ((F32, MEMORY_ROWS),))))

        k_off = D_MODEL + Q_WIDTH
        wk_t = w_in[l][:, k_off:k_off + K_WIDTH].T.astype(BF16)
        u_p, q_p, v_p, v_pb, qx_p, gl_p, kt_p, kt_pb = _rms_proj(
            xp, g_mix[l], w_in_bf, _in_splits(BF16, VALUE_ROWS, with_keys=False), wt_bf=wk_t,
            transposed=(F32, BF16), tokens_per_seq=s)
        pooled_p = _pool_prompt(u_p.reshape(b, s, d)).reshape(n_p, d)
        adiff_p = _attn_prompt(q_p.reshape(b, s, Q_WIDTH), kt_pb, v_pb.reshape(b, s, 2 * V_WIDTH), lam_vecs,
                               g_subln[l], lam_init).reshape(n_p, d)
        across_p = _cross(qx_p.reshape(b, s, d), mk, mv, n_mem, 0, BF16).reshape(n_p, d)
        x1_p, h2_p, lg_p = _merge(pooled_p, adiff_p, across_p, gl_p, xp, wts)

        u_s, q_s, k_s, v_s, _, qx_s, gl_s = _rms_proj(xs, g_mix[l], w_in_bf, _in_splits(F32, PLAIN, with_keys=True))
        pooled_s = _pool_sample(state_pool[l], u_s.reshape(db, t, d)).reshape(n_s, d)
        adiff_s = _attn_sample(q_s.reshape(db, t, Q_WIDTH), k_s.reshape(db, t, K_WIDTH), v_s.reshape(db, t, V_WIDTH),
                               cache_kt, cache_v2, page, page_table + l * n_phys, lam_vecs, g_subln[l],
                               lam_init).reshape(n_s, d)
        across_s = _cross(qx_s.reshape(db, t, d), cmk, cmv, n_mem, l * db, F32).reshape(n_s, d)
        x1_s, h2_s, lg_s = _merge(pooled_s, adiff_s, across_s, gl_s, xs, wts)

        x1 = jnp.concatenate([x1_p, x1_s], axis=0)
        h2 = jnp.concatenate([h2_p, h2_s], axis=0)
        gate, dest_flat, block_e, n_valid, zero_rows, n_slots = _route(jnp.concatenate([lg_p, lg_s], axis=0))
        out_buf = _experts(_dispatch(h2, dest_flat, zero_rows, n_slots), block_e, n_valid, w_up[l],
                           b_up[l].reshape(N_EXPERTS, 1, 2 * D_FF), w_down[l], b_down[l].reshape(N_EXPERTS, 1, d))
        y = _combine(x1, out_buf, dest_flat, gate, g_final)
        xp, xs = y[:n_p], y[n_p:]

        outs["kp"].append(kt_p.reshape(b, N_KV_HEADS, 2, HEAD_DIM, s).transpose(0, 4, 1, 2, 3))
        outs["vp"].append(v_p.reshape(b, s, N_KV_HEADS, V_DIM))
        outs["mk"].append(_memory_from_rows(mk, 1, b, n_mem)[0])
        outs["mv"].append(_memory_from_rows(mv, 1, b, n_mem)[0])
        outs["hp"].append(u_p.reshape(b, s, d)[:, -POOL_STATE:].astype(state_pool.dtype))
        outs["ks"].append(k_s.reshape(db, t, N_KV_HEADS, 2, HEAD_DIM))
        outs["vs"].append(v_s.reshape(db, t, N_KV_HEADS, V_DIM))
        outs["hs"].append(jnp.concatenate([state_pool[l], u_s.reshape(db, t, d).astype(state_pool.dtype)],
                                          axis=1)[:, -POOL_STATE:])

    return (xp.reshape(b, s, d), xs.reshape(db, t, d), jnp.stack(outs["kp"]), jnp.stack(outs["vp"]),
            jnp.stack(outs["mk"]), jnp.stack(outs["mv"]), jnp.stack(outs["hp"]), jnp.stack(outs["ks"]),
            jnp.stack(outs["vs"]), jnp.stack(outs["hs"]))
```

```python
import functools
import math

import numpy as np
import jax
import jax.numpy as jnp
from jax import lax
from jax.experimental import pallas as pl
from jax.experimental.pallas import tpu as pltpu

F32 = jnp.float32
BF16 = jnp.bfloat16

D_MODEL = 1024
N_HEADS = 8
HEAD_DIM = 64
N_KV_HEADS = 4
GQA_GROUP = N_HEADS // N_KV_HEADS
V_DIM = 2 * HEAD_DIM
POOL_WINDOWS = (2, 4, 8, 16)
POOL_GROUPS = 4
POOL_GROUP_DIM = D_MODEL // POOL_GROUPS
POOL_STATE = 15
POOL_HALO = 16
X_HEADS = 4
X_DIM = D_MODEL // X_HEADS
N_BRANCH = 3
N_EXPERTS = 32
TOP_K = 4
D_FF = D_MODEL
SWIGLU_LIMIT = 7.0
SWIGLU_ALPHA = 1.702
RMS_EPS = 1e-6
Q_WIDTH = N_HEADS * 2 * HEAD_DIM
K_WIDTH = N_KV_HEADS * 2 * HEAD_DIM
V_WIDTH = N_KV_HEADS * V_DIM
KV_BLOCK = 2 * HEAD_DIM
MASK_VALUE = -1e30

LANES = 128
EXPERT_LANES = LANES
MIB = 2 ** 20

ROW_TILE = 256
ATTN_TQ = 512
ATTN_TK = 512
PAGES_PER_STEP = 16
CROSS_TQ = 512
MOE_ROWS = 512
DISPATCH_TILE = 1024
COMBINE_TILE = 512
POOL_SEQ_BLOCK = 8


def _cparams(semantics, vmem_mib):
    return pltpu.CompilerParams(dimension_semantics=semantics, vmem_limit_bytes=vmem_mib * MIB)


def _const_spec(shape):
    zeros = (0,) * len(shape)
    return pl.BlockSpec(shape, lambda *_: zeros)


def _rms(xf, g):
    return xf * lax.rsqrt(jnp.mean(xf * xf, axis=-1, keepdims=True) + RMS_EPS) * g


TOKEN_TILE_ROWS = D_MODEL // LANES


def _store_token_tiles(ref, x):
    n = x.shape[0]
    for s in range(TOKEN_TILE_ROWS):
        ref[pl.ds(s, n, stride=TOKEN_TILE_ROWS), :] = x[:, s * LANES:(s + 1) * LANES]


def _load_token_tiles(ref, n, lead=()):
    return [ref[lead + (pl.ds(s, n, stride=TOKEN_TILE_ROWS), slice(None))] for s in range(TOKEN_TILE_ROWS)]


PLAIN = ("plain",)
WITH_ONES = ("with_ones",)


def _chunk_rows(positions):
    return ("chunk_rows", tuple(positions))


VALUE_ROWS = _chunk_rows(range(N_KV_HEADS))
MEMORY_ROWS = _chunk_rows([(q % 2) * X_HEADS + q // 2 for q in range(X_HEADS * 2)])


def _rms_proj_kernel(x_ref, g_ref, w_ref, *rest, splits, transposed):
    hb = _rms(x_ref[...].astype(F32), g_ref[...]).astype(BF16)
    if transposed:
        wt_ref, rest = rest[0], rest[1:]
    out_refs = rest
    oi = 0
    for c0, width, scale, copies in splits:
        z = jnp.dot(hb, w_ref[:, c0:c0 + width], preferred_element_type=F32)
        if scale != 1.0:
            z = z * scale
        for dt, layout in copies:
            o_ref = out_refs[oi]
            oi += 1
            if layout == WITH_ONES:
                for h in range(width // V_DIM):
                    o_ref[:, 2 * h * V_DIM:(2 * h + 1) * V_DIM] = z[:, h * V_DIM:(h + 1) * V_DIM].astype(dt)
                    o_ref[:, (2 * h + 1) * V_DIM:(2 * h + 2) * V_DIM] = jnp.ones((z.shape[0], V_DIM), dt)
            elif layout[0] == "chunk_rows":
                positions = layout[1]
                for q, pos in enumerate(positions):
                    o_ref[pl.ds(pos, z.shape[0], stride=len(positions)), :] = z[:, q * LANES:(q + 1) * LANES].astype(dt)
            else:
                o_ref[...] = z.astype(dt)
    if transposed:
        zt = lax.dot_general(wt_ref[...], hb, (((1,), (1,)), ((), ())), preferred_element_type=F32)
        for dt in transposed:
            out_refs[oi][...] = zt.astype(dt)
            oi += 1


def _rms_proj(x2d, g, w_bf, splits, wt_bf=None, transposed=(), tokens_per_seq=None):
    n, d = x2d.shape
    tm = ROW_TILE
    assert n % tm == 0
    out_shape, out_specs = [], []
    for _, width, _, copies in splits:
        for dt, layout in copies:
            if layout[0] == "chunk_rows":
                n_chunks = len(layout[1])
                assert width == n_chunks * LANES
                shape, block = (n * n_chunks, LANES), (tm * n_chunks, LANES)
            else:
                w_out = 2 * width if layout == WITH_ONES else width
                shape, block = (n, w_out), (tm, w_out)
            out_shape.append(jax.ShapeDtypeStruct(shape, dt))
            out_specs.append(pl.BlockSpec(block, lambda i: (i, 0)))
    operands = [x2d, g.reshape(1, d), w_bf]
    in_specs = [pl.BlockSpec((tm, d), lambda i: (i, 0)), _const_spec((1, d)), _const_spec(w_bf.shape)]
    if transposed:
        assert tokens_per_seq % tm == 0 and n % tokens_per_seq == 0
        tiles_per_seq = tokens_per_seq // tm
        feats = wt_bf.shape[0]
        operands.append(wt_bf)
        in_specs.append(_const_spec(wt_bf.shape))
        for dt in transposed:
            out_shape.append(jax.ShapeDtypeStruct((n // tokens_per_seq, feats, tokens_per_seq), dt))
            out_specs.append(pl.BlockSpec((None, feats, tm), lambda i: (i // tiles_per_seq, 0, i % tiles_per_seq)))
    return pl.pallas_call(
        functools.partial(_rms_proj_kernel, splits=splits, transposed=tuple(transposed)),
        grid=(n // tm,),
        in_specs=in_specs,
        out_specs=out_specs,
        out_shape=out_shape,
        compiler_params=_cparams(("parallel",), 56),
        name="rms_proj",
    )(*operands)


def _pool_prompt_kernel(u_ref, halo_ref, o_ref, ue, *, tm):
    i = pl.program_id(1)
    ue[0:POOL_HALO, :] = jnp.where(i == 0, 0.0, halo_ref[...])
    ue[POOL_HALO:, :] = u_ref[...]
    pos = i * tm + lax.broadcasted_iota(jnp.int32, (tm, 1), 0)
    for g, w in enumerate(POOL_WINDOWS):
        c = slice(g * POOL_GROUP_DIM, (g + 1) * POOL_GROUP_DIM)
        tok = ue[POOL_HALO:POOL_HALO + tm, c]
        acc = tok
        for j in range(1, w):
            acc = acc + ue[POOL_HALO - j:POOL_HALO - j + tm, c]
        cnt = jnp.minimum(pos + 1, w).astype(F32)
        o_ref[:, c] = (acc / cnt - tok).astype(o_ref.dtype)


def _pool_prompt(u3):
    b, s, d = u3.shape
    tm = ROW_TILE
    halo_blocks = tm // POOL_HALO
    return pl.pallas_call(
        functools.partial(_pool_prompt_kernel, tm=tm),
        grid=(b, s // tm),
        in_specs=[pl.BlockSpec((None, tm, d), lambda bi, i: (bi, i, 0)),
                  pl.BlockSpec((None, POOL_HALO, d), lambda bi, i: (bi, jnp.maximum(i * halo_blocks - 1, 0), 0))],
        out_specs=pl.BlockSpec((None, tm, d), lambda bi, i: (bi, i, 0)),
        out_shape=jax.ShapeDtypeStruct((b, s, d), BF16),
        scratch_shapes=[pltpu.VMEM((tm + POOL_HALO, d), F32)],
        compiler_params=_cparams(("parallel", "parallel"), 32),
        name="pool_prompt",
    )(u3, u3)


def _pool_sample_kernel(ue_ref, o_ref, *, n_new):
    lo = ue_ref.shape[1] - n_new
    for s in range(ue_ref.shape[0]):
        for g, w in enumerate(POOL_WINDOWS):
            c = slice(g * POOL_GROUP_DIM, (g + 1) * POOL_GROUP_DIM)
            tok = ue_ref[s, lo:lo + n_new, c]
            acc = tok
            for j in range(1, w):
                acc = acc + ue_ref[s, lo - j:lo - j + n_new, c]
            o_ref[s, :, c] = acc / float(w) - tok


def _pool_sample(hist, u3):
    db, t, d = u3.shape
    assert hist.shape[1] >= POOL_WINDOWS[-1] - 1
    pad = (-(hist.shape[1] + t)) % 8
    ue = jnp.concatenate([jnp.zeros((db, pad, d), F32), hist.astype(F32), u3], axis=1)
    rows = ue.shape[1]
    sb = POOL_SEQ_BLOCK
    return pl.pallas_call(
        functools.partial(_pool_sample_kernel, n_new=t),
        grid=(db // sb,),
        in_specs=[pl.BlockSpec((sb, rows, d), lambda i: (i, 0, 0))],
        out_specs=pl.BlockSpec((sb, t, d), lambda i: (i, 0, 0)),
        out_shape=jax.ShapeDtypeStruct((db, t, d), F32),
        compiler_params=_cparams(("parallel",), 32),
        name="pool_sample",
    )(ue)


def _lambda_value(lam_ref, lam_init):
    a = jnp.sum(lam_ref[0:1, :] * lam_ref[1:2, :], axis=-1, keepdims=True)
    b = jnp.sum(lam_ref[2:3, :] * lam_ref[3:4, :], axis=-1, keepdims=True)
    return jnp.exp(a) - jnp.exp(b) + lam_init


def _subln(o, gsub, lam_init):
    return _rms(o, gsub) * (1.0 - lam_init)


def _softmax_update(s, m_prev):
    m_new = jnp.maximum(m_prev, jnp.max(s, axis=-1, keepdims=True))
    alpha = jnp.exp2(m_prev - m_new)
    p = jnp.exp2(s - jnp.tile(m_new, (1, s.shape[1] // LANES)))
    return p, m_new, alpha


def _attn_prompt_kernel(qi_ref, ki_ref, last_ref, q_ref, k_ref, v_ref, lam_ref, gsub_ref, o_ref,
                        qpad, acc, m_scr, *, tq, tk, lam_init):
    step = pl.program_id(1)
    qi = qi_ref[step]
    ki = ki_ref[step]
    n_streams = 2 * N_KV_HEADS

    @pl.when(ki == 0)
    def _init():
        lane = lax.broadcasted_iota(jnp.int32, (tq, KV_BLOCK), 1)
        for h in range(N_KV_HEADS):
            for g in range(GQA_GROUP):
                c0 = (h * GQA_GROUP + g) * KV_BLOCK
                blk = q_ref[:, c0:c0 + KV_BLOCK]
                rows = slice(g * tq, (g + 1) * tq)
                qpad[2 * h, rows, :] = jnp.where(lane < HEAD_DIM, blk, jnp.zeros_like(blk))
                qpad[2 * h + 1, rows, :] = jnp.where(lane >= HEAD_DIM, blk, jnp.zeros_like(blk))
        acc[...] = jnp.zeros_like(acc)
        m_scr[...] = jnp.full_like(m_scr, MASK_VALUE)

    items = [(idx, g) for idx in range(n_streams) for g in range(GQA_GROUP)]

    def scores(item):
        idx, g = item
        h = idx // 2
        return jnp.dot(qpad[idx, g * tq:(g + 1) * tq, :], k_ref[h * KV_BLOCK:(h + 1) * KV_BLOCK, :],
                       preferred_element_type=F32)

    def body(masked):
        if masked:
            qpos = qi * tq + lax.broadcasted_iota(jnp.int32, (tq, tk), 0)
            kpos = ki * tk + lax.broadcasted_iota(jnp.int32, (tq, tk), 1)
            vis = kpos <= qpos
        s_next = scores(items[0])
        for n, (idx, g) in enumerate(items):
            s = s_next
            if n + 1 < len(items):
                s_next = scores(items[n + 1])
            if masked:
                s = jnp.where(vis, s, MASK_VALUE)
            rows = slice(g * tq, (g + 1) * tq)
            p, m_new, alpha = _softmax_update(s, m_scr[idx, rows, :])
            m_scr[idx, rows, :] = m_new
            h = idx // 2
            pv = jnp.dot(p.astype(BF16), v_ref[:, h * 2 * V_DIM:(h + 1) * 2 * V_DIM], preferred_element_type=F32)
            acc[idx, rows, :] = jnp.tile(alpha, (1, 2)) * acc[idx, rows, :] + pv

    needs_mask = (ki + 1) * tk - 1 > qi * tq

    @pl.when(needs_mask)
    def _masked():
        body(True)

    @pl.when(jnp.logical_not(needs_mask))
    def _unmasked():
        body(False)

    @pl.when(last_ref[step] == 1)
    def _finish():
        lam = _lambda_value(lam_ref, lam_init)
        gsub = gsub_ref[...]
        for h in range(N_KV_HEADS):
            for g in range(GQA_GROUP):
                rows = slice(g * tq, (g + 1) * tq)
                o1 = acc[2 * h, rows, 0:V_DIM] / acc[2 * h, rows, V_DIM:2 * V_DIM]
                o2 = acc[2 * h + 1, rows, 0:V_DIM] / acc[2 * h + 1, rows, V_DIM:2 * V_DIM]
                c0 = (h * GQA_GROUP + g) * V_DIM
                o_ref[:, c0:c0 + V_DIM] = _subln(o1 - lam * o2, gsub, lam_init).astype(o_ref.dtype)


def _attn_prompt(q, k, v, lam_vecs, gsub, lam_init):
    b, s, _ = q.shape
    tq, tk = ATTN_TQ, ATTN_TK
    assert s % tq == 0 and s % tk == 0
    qi_l, ki_l, last_l = [], [], []
    for qi in range(s // tq):
        n_k = min(((qi + 1) * tq - 1) // tk + 1, s // tk)
        for ki in range(n_k):
            qi_l.append(qi)
            ki_l.append(ki)
            last_l.append(1 if ki == n_k - 1 else 0)
    tabs = [jnp.asarray(np.array(t, np.int32)) for t in (qi_l, ki_l, last_l)]
    rows = GQA_GROUP * tq
    grid_spec = pltpu.PrefetchScalarGridSpec(
        num_scalar_prefetch=3,
        grid=(b, len(qi_l)),
        in_specs=[pl.BlockSpec((None, tq, Q_WIDTH), lambda bi, st, qt, kt, lt: (bi, qt[st], 0)),
                  pl.BlockSpec((None, K_WIDTH, tk), lambda bi, st, qt, kt, lt: (bi, 0, kt[st])),
                  pl.BlockSpec((None, tk, 2 * V_WIDTH), lambda bi, st, qt, kt, lt: (bi, kt[st], 0)),
                  _const_spec((4, HEAD_DIM)), _const_spec((1, V_DIM))],
        out_specs=pl.BlockSpec((None, tq, N_HEADS * V_DIM), lambda bi, st, qt, kt, lt: (bi, qt[st], 0)),
        scratch_shapes=[pltpu.VMEM((2 * N_KV_HEADS, rows, KV_BLOCK), BF16),
                        pltpu.VMEM((2 * N_KV_HEADS, rows, 2 * V_DIM), F32),
                        pltpu.VMEM((2 * N_KV_HEADS, rows, LANES), F32)])
    return pl.pallas_call(
        functools.partial(_attn_prompt_kernel, tq=tq, tk=tk, lam_init=lam_init),
        grid_spec=grid_spec,
        out_shape=jax.ShapeDtypeStruct((b, s, N_HEADS * V_DIM), BF16),
        compiler_params=_cparams(("parallel", "arbitrary"), 56),
        name="attn_prompt",
    )(*tabs, q, k, v, lam_vecs, gsub.reshape(1, V_DIM))


SAMPLE_ROWS = N_KV_HEADS * 2 * GQA_GROUP


def _attn_sample_kernel(pt_ref, wq_ref, kn_ref, vn_ref, lam_ref, gsub_ref, kt_hbm, v_hbm, o_ref,
                        kbuf, vbuf, acc, m_scr, l_scr, sem_k, sem_v, *, n_pages, t_new, page, lam_init):
    seq = pl.program_id(0)
    step = pl.program_id(1)
    n_steps = pl.num_programs(1)
    rows = SAMPLE_ROWS * t_new
    k_rows = kbuf.shape[1] // n_pages
    v_rows = vbuf.shape[1] // n_pages
    flat = seq * n_steps + step
    slot = flat % 2

    def page_copies(seq_i, step_i, slot_i):
        copies = []
        for j in range(n_pages):
            pid = pt_ref[seq_i, step_i * n_pages + j]
            copies.append(pltpu.make_async_copy(kt_hbm.at[pl.ds(pl.multiple_of(pid * k_rows, k_rows), k_rows), :],
                                                kbuf.at[slot_i, pl.ds(j * k_rows, k_rows), :], sem_k.at[slot_i]))
            copies.append(pltpu.make_async_copy(v_hbm.at[pl.ds(pl.multiple_of(pid * v_rows, v_rows), v_rows), :],
                                                vbuf.at[slot_i, pl.ds(j * v_rows, v_rows), :], sem_v.at[slot_i]))
        return copies

    @pl.when(flat == 0)
    def _first_fetch():
        for c in page_copies(seq, step, slot):
            c.start()

    @pl.when(flat + 1 < pl.num_programs(0) * n_steps)
    def _prefetch():
        wrap = step + 1 == n_steps
        for c in page_copies(jnp.where(wrap, seq + 1, seq), jnp.where(wrap, 0, step + 1), 1 - slot):
            c.start()

    @pl.when(step == 0)
    def _init():
        acc[...] = jnp.zeros_like(acc)
        m_scr[...] = jnp.full_like(m_scr, MASK_VALUE)
        l_scr[...] = jnp.zeros_like(l_scr)

    pltpu.make_async_copy(kt_hbm.at[pl.ds(0, n_pages * k_rows), :], kbuf.at[slot], sem_k.at[slot]).wait()
    pltpu.make_async_copy(v_hbm.at[pl.ds(0, n_pages * v_rows), :], vbuf.at[slot], sem_v.at[slot]).wait()

    wq = wq_ref[...]

    def update(s, v_list):
        p, m_new, alpha = _softmax_update(s, m_scr[...])
        m_scr[...] = m_new
        l_scr[...] = alpha * l_scr[...] + jnp.sum(p, axis=-1, keepdims=True)
        pb = p.astype(BF16)
        pv = None
        for j, vj in enumerate(v_list):
            c = vj.shape[0]
            term = jnp.dot(pb[:, j * c:(j + 1) * c], vj, preferred_element_type=F32)
            pv = term if pv is None else pv + term
        acc[...] = jnp.tile(alpha, (1, N_KV_HEADS)) * acc[...] + pv

    def page_keys(j):
        return kbuf[slot, pl.ds(j * k_rows, k_rows), :].astype(BF16)

    def page_values(j):
        heads = [vbuf[slot, pl.ds(j * v_rows + h, page, stride=N_KV_HEADS), :] for h in range(N_KV_HEADS)]
        return jnp.concatenate(heads, axis=1).astype(BF16)

    s_list = [jnp.dot(wq, page_keys(j), preferred_element_type=F32) for j in range(n_pages)]
    update(jnp.concatenate(s_list, axis=1), [page_values(j) for j in range(n_pages)])

    @pl.when(step == pl.num_programs(1) - 1)
    def _finish():
        pad = jnp.zeros((LANES - t_new, kn_ref.shape[1]), F32)
        kn = jnp.concatenate([kn_ref[...], pad], axis=0).astype(BF16)
        vn = jnp.concatenate([vn_ref[...], pad], axis=0).astype(BF16)
        s_new = lax.dot_general(wq, kn, (((1,), (1,)), ((), ())), preferred_element_type=F32)
        t_of_row = lax.broadcasted_iota(jnp.int32, (rows, LANES), 0) % t_new
        key = lax.broadcasted_iota(jnp.int32, (rows, LANES), 1)
        update(jnp.where(key <= t_of_row, s_new, MASK_VALUE), [vn])

        lam = _lambda_value(lam_ref, lam_init)
        gsub = gsub_ref[...]
        for h in range(N_KV_HEADS):
            for g in range(GQA_GROUP):
                r1 = (h * 2 * GQA_GROUP + g) * t_new
                r2 = (h * 2 * GQA_GROUP + GQA_GROUP + g) * t_new
                cols = slice(h * V_DIM, (h + 1) * V_DIM)
                o1 = acc[r1:r1 + t_new, cols] / l_scr[r1:r1 + t_new, :]
                o2 = acc[r2:r2 + t_new, cols] / l_scr[r2:r2 + t_new, :]
                c0 = (h * GQA_GROUP + g) * V_DIM
                o_ref[:, c0:c0 + V_DIM] = _subln(o1 - lam * o2, gsub, lam_init).astype(o_ref.dtype)


def _sample_query_rows(q3):
    db, t, _ = q3.shape
    q6 = q3.reshape(db, t, N_KV_HEADS, GQA_GROUP, 2, HEAD_DIM).transpose(0, 2, 4, 3, 1, 5)
    eye = jnp.eye(N_KV_HEADS * 2, dtype=q3.dtype).reshape(N_KV_HEADS, 2, N_KV_HEADS, 2)
    wq = jnp.einsum("bhmgtd,hmHM->bhmgtHMd", q6, eye)
    return wq.reshape(db, SAMPLE_ROWS * t, K_WIDTH).astype(BF16)


def _attn_sample(q3, k_new, v_new, cache_kt, cache_v, page, page_ids, lam_vecs, gsub, lam_init):
    db, t, _ = q3.shape
    n_pages_total = page_ids.shape[1]
    g = PAGES_PER_STEP
    rows = SAMPLE_ROWS * t
    assert n_pages_total % g == 0 and rows % 8 == 0 and t <= LANES and page % LANES == 0
    wq = _sample_query_rows(q3)

    assert page == LANES and cache_kt.shape[1] == page and cache_v.shape[1] == V_DIM
    grid_spec = pltpu.PrefetchScalarGridSpec(
        num_scalar_prefetch=1,
        grid=(db, n_pages_total // g),
        in_specs=[pl.BlockSpec((None, rows, K_WIDTH), lambda bi, st, pt: (bi, 0, 0)),
                  pl.BlockSpec((None, t, K_WIDTH), lambda bi, st, pt: (bi, 0, 0)),
                  pl.BlockSpec((None, t, V_WIDTH), lambda bi, st, pt: (bi, 0, 0)),
                  _const_spec((4, HEAD_DIM)), _const_spec((1, V_DIM)),
                  pl.BlockSpec(memory_space=pl.ANY), pl.BlockSpec(memory_space=pl.ANY)],
        out_specs=pl.BlockSpec((None, t, N_HEADS * V_DIM), lambda bi, st, pt: (bi, 0, 0)),
        scratch_shapes=[pltpu.VMEM((2, g * K_WIDTH, page), F32),
                        pltpu.VMEM((2, g * page * N_KV_HEADS, V_DIM), F32),
                        pltpu.VMEM((rows, V_WIDTH), F32),
                        pltpu.VMEM((rows, LANES), F32),
                        pltpu.VMEM((rows, LANES), F32),
                        pltpu.SemaphoreType.DMA((2,)), pltpu.SemaphoreType.DMA((2,))])
    return pl.pallas_call(
        functools.partial(_attn_sample_kernel, n_pages=g, t_new=t, page=page, lam_init=lam_init),
        grid_spec=grid_spec,
        out_shape=jax.ShapeDtypeStruct((db, t, N_HEADS * V_DIM), F32),
        compiler_params=_cparams(("arbitrary", "arbitrary"), 40),
        name="attn_sample",
    )(page_ids, wq, k_new, v_new, lam_vecs, gsub.reshape(1, V_DIM), cache_kt, cache_v)


def _cross_kernel(q_ref, mk_ref, mv_ref, o_ref, *, n_mem):
    halves = X_DIM // LANES
    rows_per_token = X_HEADS * halves

    def head(ref, h):
        parts = [ref[pl.ds(c * X_HEADS + h, n_mem, stride=rows_per_token), :] for c in range(halves)]
        return jnp.concatenate(parts, axis=1).astype(BF16)

    for h in range(X_HEADS):
        c = slice(h * X_DIM, (h + 1) * X_DIM)
        q = q_ref[:, c].astype(BF16)
        s = lax.dot_general(q, head(mk_ref, h), (((1,), (1,)), ((), ())), preferred_element_type=F32)
        p = jnp.exp(s - jnp.max(s, axis=-1, keepdims=True))
        l = jnp.sum(p, axis=-1, keepdims=True)
        o = jnp.dot(p.astype(BF16), head(mv_ref, h), preferred_element_type=F32)
        o_ref[:, c] = (o / l).astype(o_ref.dtype)


def _memory_rows(mem5):
    l, b, n_mem, heads, xd = mem5.shape
    m6 = mem5.reshape(l, b, n_mem, heads, xd // LANES, LANES).transpose(0, 1, 2, 4, 3, 5)
    return m6.reshape(l * b * n_mem * heads * (xd // LANES), LANES)


def _memory_from_rows(rows, l, b, n_mem):
    halves = X_DIM // LANES
    return rows.reshape(l, b, n_mem, halves, X_HEADS, LANES).transpose(0, 1, 2, 4, 3, 5).reshape(
        l, b, n_mem, X_HEADS, X_DIM)


def _cross(q3, mk, mv, n_mem, mem_offset, out_dtype):
    b, t, d = q3.shape
    tq = min(CROSS_TQ, t)
    mem_rows = n_mem * (d // LANES)
    return pl.pallas_call(
        functools.partial(_cross_kernel, n_mem=n_mem),
        grid=(b, t // tq),
        in_specs=[pl.BlockSpec((None, tq, d), lambda bi, i: (bi, i, 0)),
                  pl.BlockSpec((mem_rows, LANES), lambda bi, i: (bi + mem_offset, 0)),
                  pl.BlockSpec((mem_rows, LANES), lambda bi, i: (bi + mem_offset, 0))],
        out_specs=pl.BlockSpec((None, tq, d), lambda bi, i: (bi, i, 0)),
        out_shape=jax.ShapeDtypeStruct((b, t, d), out_dtype),
        compiler_params=_cparams(("parallel", "parallel"), 32),
        name="cross_attn",
    )(q3, mk, mv)


N_MERGE_INPUTS = 5


def _merge_kernel(*refs, first_tiles):
    first, second = refs[:N_MERGE_INPUTS], refs[N_MERGE_INPUTS:2 * N_MERGE_INPUTS]
    (wg_ref, ps_ref, wpp_ref, wdp_ref, wcp_ref, wo_ref, gffn_ref, wr_ref, br_ref,
     x1_ref, h2_ref, lg_ref) = refs[2 * N_MERGE_INPUTS:]

    def body(pooled_ref, adiff_ref, across_ref, gl_ref, x_ref):
        pooled = pooled_ref[...].astype(BF16)
        ys = [jnp.dot(pooled[:, g * POOL_GROUP_DIM:(g + 1) * POOL_GROUP_DIM], wg_ref[g],
                      preferred_element_type=F32) for g in range(POOL_GROUPS)]
        a_pool = jnp.concatenate(ys, axis=1) * ps_ref[...]
        p_pool = jnp.dot(a_pool.astype(BF16), wpp_ref[...], preferred_element_type=F32)
        p_diff = jnp.dot(adiff_ref[...].astype(BF16), wdp_ref[...], preferred_element_type=F32)
        p_cross = jnp.dot(across_ref[...].astype(BF16), wcp_ref[...], preferred_element_type=F32)
        merged = (jax.nn.sigmoid(gl_ref[:, 0:D_MODEL]) * p_pool
                  + jax.nn.sigmoid(gl_ref[:, D_MODEL:2 * D_MODEL]) * p_diff
                  + jax.nn.sigmoid(gl_ref[:, 2 * D_MODEL:3 * D_MODEL]) * p_cross)
        x1 = x_ref[...] + jnp.dot(merged.astype(BF16), wo_ref[...], preferred_element_type=F32)
        x1_ref[...] = x1
        h2 = _rms(x1, gffn_ref[...])
        _store_token_tiles(h2_ref, h2)
        lg_ref[...] = jnp.dot(h2.astype(BF16), wr_ref[...], preferred_element_type=F32) + br_ref[...]

    i = pl.program_id(0)

    @pl.when(i < first_tiles)
    def _first():
        body(*first)

    @pl.when(i >= first_tiles)
    def _second():
        body(*second)


def _merge(first, second, wts):
    d = D_MODEL
    tm = ROW_TILE
    n1, n2 = first[-1].shape[0], second[-1].shape[0]
    assert n1 % tm == 0 and n2 % tm == 0
    t1 = n1 // tm
    n = n1 + n2
    widths = (d, d, d, N_BRANCH * d, d)
    first_specs = [pl.BlockSpec((tm, w), lambda i: (jnp.minimum(i, t1 - 1), 0)) for w in widths]
    second_specs = [pl.BlockSpec((tm, w), lambda i: (jnp.maximum(i - t1, 0), 0)) for w in widths]
    row = lambda w: pl.BlockSpec((tm, w), lambda i: (i, 0))
    consts = [wts["wg"], wts["pool_scale"], wts["wpp"], wts["wdp"], wts["wcp"], wts["wo"], wts["g_ffn"],
              wts["wr"], wts["br"]]
    return pl.pallas_call(
        functools.partial(_merge_kernel, first_tiles=t1),
        grid=(n // tm,),
        in_specs=first_specs + second_specs + [_const_spec(c.shape) for c in consts],
        out_specs=[row(d), pl.BlockSpec((tm * TOKEN_TILE_ROWS, LANES), lambda i: (i, 0)), row(EXPERT_LANES)],
        out_shape=[jax.ShapeDtypeStruct((n, d), F32), jax.ShapeDtypeStruct((n * TOKEN_TILE_ROWS, LANES), F32),
                   jax.ShapeDtypeStruct((n, EXPERT_LANES), F32)],
        compiler_params=_cparams(("arbitrary",), 56),
        name="merge",
    )(*first, *second, *consts)


def _expert_kernel(be_ref, nv_ref, xs_ref, wu_ref, bu_ref, wd_ref, bd_ref, o_ref, wu_bf, wd_bf):
    i = pl.program_id(0)
    new_expert = jnp.logical_or(i == 0, be_ref[i] != be_ref[jnp.maximum(i - 1, 0)])

    @pl.when(jnp.logical_and(i < nv_ref[0], new_expert))
    def _cast_weights():
        wu_bf[...] = wu_ref[...].astype(BF16)
        wd_bf[...] = wd_ref[...].astype(BF16)

    @pl.when(i < nv_ref[0])
    def _run():
        x = jnp.concatenate(_load_token_tiles(xs_ref, MOE_ROWS), axis=1).astype(BF16)
        up = jnp.dot(x, wu_bf[...], preferred_element_type=F32) + bu_ref[...]
        glu = jnp.minimum(up[:, :D_FF], SWIGLU_LIMIT)
        lin = jnp.clip(up[:, D_FF:], -SWIGLU_LIMIT, SWIGLU_LIMIT)
        act = glu * jax.nn.sigmoid(SWIGLU_ALPHA * glu) * (lin + 1.0)
        _store_token_tiles(o_ref, jnp.dot(act.astype(BF16), wd_bf[...], preferred_element_type=F32) + bd_ref[...])

    @pl.when(i >= nv_ref[0])
    def _skip():
        o_ref[...] = jnp.zeros_like(o_ref)


def _experts(xs, block_e, n_valid, wu, bu, wd, bd):
    d = D_MODEL
    tile_rows = MOE_ROWS * TOKEN_TILE_ROWS
    grid_spec = pltpu.PrefetchScalarGridSpec(
        num_scalar_prefetch=2,
        grid=(xs.shape[0] // tile_rows,),
        in_specs=[pl.BlockSpec((tile_rows, LANES), lambda i, be, nv: (jnp.minimum(i, nv[0] - 1), 0)),
                  pl.BlockSpec((None, d, 2 * D_FF), lambda i, be, nv: (be[i], 0, 0)),
                  pl.BlockSpec((None, 1, 2 * D_FF), lambda i, be, nv: (be[i], 0, 0)),
                  pl.BlockSpec((None, D_FF, d), lambda i, be, nv: (be[i], 0, 0)),
                  pl.BlockSpec((None, 1, d), lambda i, be, nv: (be[i], 0, 0))],
        out_specs=pl.BlockSpec((tile_rows, LANES), lambda i, be, nv: (i, 0)),
        scratch_shapes=[pltpu.VMEM((d, 2 * D_FF), BF16), pltpu.VMEM((D_FF, d), BF16)])
    return pl.pallas_call(
        _expert_kernel,
        grid_spec=grid_spec,
        out_shape=jax.ShapeDtypeStruct(xs.shape, F32),
        compiler_params=_cparams(("arbitrary",), 56),
        name="experts",
    )(block_e, n_valid, xs, wu, bu, wd, bd)


def _slot_tile_copy(dest_hbm, dsm, sem, i, td):
    start = pl.multiple_of(i * (td * TOP_K), td * TOP_K)
    return pltpu.make_async_copy(dest_hbm.at[pl.ds(start, td * TOP_K)], dsm, sem)


def _token_rows(ref, idx, lead=()):
    start = pl.multiple_of(idx * TOKEN_TILE_ROWS, TOKEN_TILE_ROWS)
    return ref.at[lead + (pl.ds(start, TOKEN_TILE_ROWS), slice(None))]


def _combine_kernel(dest_hbm, ob_hbm, x1_ref, gate_ref, gf_ref, y1_ref, y2_ref, dsm, rows, sem_d, sem_r, *,
                    td, first_tiles):
    i = pl.program_id(0)
    slots = _slot_tile_copy(dest_hbm, dsm, sem_d, i, td)
    slots.start()
    slots.wait()

    def issue(r, carry):
        for k in range(TOP_K):
            pltpu.make_async_copy(_token_rows(ob_hbm, dsm[r * TOP_K + k]), _token_rows(rows, r, (k,)),
                                  sem_r).start(priority=k % 2)
        return carry

    lax.fori_loop(0, td, issue, 0, unroll=8)
    for k in range(TOP_K):
        pltpu.make_async_copy(ob_hbm.at[pl.ds(0, td * TOKEN_TILE_ROWS), :], rows.at[k], sem_r).wait()

    gate = gate_ref[...]
    parts = [_load_token_tiles(rows, td, (k,)) for k in range(TOP_K)]
    pieces = []
    for s in range(TOKEN_TILE_ROWS):
        moe = gate[:, 0:1] * parts[0][s]
        for k in range(1, TOP_K):
            moe = moe + gate[:, k:k + 1] * parts[k][s]
        pieces.append(x1_ref[:, s * LANES:(s + 1) * LANES] + moe)
    y = _rms(jnp.concatenate(pieces, axis=1), gf_ref[...])

    @pl.when(i < first_tiles)
    def _first():
        y1_ref[...] = y

    @pl.when(i >= first_tiles)
    def _second():
        y2_ref[...] = y


def _combine(x1, out_buf, dest_flat, gate, g_final, n_first):
    n, d = x1.shape
    td = COMBINE_TILE
    assert n % td == 0 and n_first % td == 0 and 0 < n_first < n
    t1 = n_first // td
    return pl.pallas_call(
        functools.partial(_combine_kernel, td=td, first_tiles=t1),
        grid=(n // td,),
        in_specs=[pl.BlockSpec(memory_space=pl.ANY), pl.BlockSpec(memory_space=pl.ANY),
                  pl.BlockSpec((td, d), lambda i: (i, 0)),
                  pl.BlockSpec((td, EXPERT_LANES), lambda i: (i, 0)),
                  _const_spec((1, d))],
        out_specs=[pl.BlockSpec((td, d), lambda i: (jnp.minimum(i, t1 - 1), 0)),
                   pl.BlockSpec((td, d), lambda i: (jnp.maximum(i - t1, 0), 0))],
        out_shape=[jax.ShapeDtypeStruct((n_first, d), F32), jax.ShapeDtypeStruct((n - n_first, d), F32)],
        scratch_shapes=[pltpu.SMEM((td * TOP_K,), jnp.int32),
                        pltpu.VMEM((TOP_K, td * TOKEN_TILE_ROWS, LANES), F32),
                        pltpu.SemaphoreType.DMA, pltpu.SemaphoreType.DMA],
        compiler_params=_cparams(("arbitrary",), 32),
        name="combine_norm",
    )(dest_flat, out_buf, x1, gate, g_final.reshape(1, d))


def _dispatch_kernel(zrow_ref, dest_hbm, h2_ref, xs_hbm, dsm, zbuf, sem_d, sem_z, sem_r, *, td):
    i = pl.program_id(0)
    block_rows = MOE_ROWS * TOKEN_TILE_ROWS

    def zero_copy(e):
        return pltpu.make_async_copy(zbuf, xs_hbm.at[pl.ds(_tile_start(zrow_ref[e]), block_rows), :], sem_z)

    def tail_copy(blk):
        return pltpu.make_async_copy(zbuf, xs_hbm.at[pl.ds(_tile_start(blk * MOE_ROWS), block_rows), :], sem_z)

    @pl.when(i == 0)
    def _clear_unwritten_blocks():
        zbuf[...] = jnp.zeros_like(zbuf)
        first_unused = zrow_ref[N_EXPERTS]
        n_blocks = xs_hbm.shape[0] // block_rows
        for e in range(N_EXPERTS):
            @pl.when(zrow_ref[e] >= 0)
            def _start(e=e):
                zero_copy(e).start()
        lax.fori_loop(first_unused, n_blocks, lambda blk, c: (tail_copy(blk).start(), c)[1], 0)
        for e in range(N_EXPERTS):
            @pl.when(zrow_ref[e] >= 0)
            def _wait(e=e):
                zero_copy(e).wait()
        lax.fori_loop(first_unused, n_blocks, lambda blk, c: (tail_copy(blk).wait(), c)[1], 0)

    slots = _slot_tile_copy(dest_hbm, dsm, sem_d, i, td)
    slots.start()
    slots.wait()

    def issue(r, carry):
        src = _token_rows(h2_ref, r)
        for k in range(TOP_K):
            pltpu.make_async_copy(src, _token_rows(xs_hbm, dsm[r * TOP_K + k]), sem_r).start(priority=k % 2)
        return carry

    lax.fori_loop(0, td, issue, 0, unroll=8)
    for _ in range(TOP_K):
        pltpu.make_async_copy(h2_ref, xs_hbm.at[pl.ds(0, td * TOKEN_TILE_ROWS), :], sem_r).wait()


def _tile_start(slot):
    return pl.multiple_of(slot * TOKEN_TILE_ROWS, TOKEN_TILE_ROWS)


def _dispatch(h2_tiles, dest_flat, zero_rows, n_slots):
    td = DISPATCH_TILE
    n = h2_tiles.shape[0] // TOKEN_TILE_ROWS
    assert n % td == 0
    grid_spec = pltpu.PrefetchScalarGridSpec(
        num_scalar_prefetch=1,
        grid=(n // td,),
        in_specs=[pl.BlockSpec(memory_space=pl.ANY),
                  pl.BlockSpec((td * TOKEN_TILE_ROWS, LANES), lambda i, zr: (i, 0))],
        out_specs=pl.BlockSpec(memory_space=pl.ANY),
        scratch_shapes=[pltpu.SMEM((td * TOP_K,), jnp.int32),
                        pltpu.VMEM((MOE_ROWS * TOKEN_TILE_ROWS, LANES), F32),
                        pltpu.SemaphoreType.DMA, pltpu.SemaphoreType.DMA, pltpu.SemaphoreType.DMA])
    return pl.pallas_call(
        functools.partial(_dispatch_kernel, td=td),
        grid_spec=grid_spec,
        out_shape=jax.ShapeDtypeStruct((n_slots * TOKEN_TILE_ROWS, LANES), F32),
        compiler_params=_cparams(("arbitrary",), 32),
        name="dispatch",
    )(zero_rows, dest_flat, h2_tiles)


def _route_kernel(lg_ref, eid_ref, gate_ref, rank_ref, cnt_ref, run_scr):
    i = pl.program_id(0)

    @pl.when(i == 0)
    def _init():
        run_scr[...] = jnp.zeros_like(run_scr)

    x = lg_ref[...]
    tm = x.shape[0]
    lane = lax.broadcasted_iota(jnp.int32, x.shape, 1).astype(F32)
    vals, hots = [], []
    eid = jnp.zeros(x.shape, F32)
    for k in range(TOP_K):
        m = jnp.max(x, axis=-1, keepdims=True)
        idx = jnp.min(jnp.where(x == m, lane, float(EXPERT_LANES)), axis=-1, keepdims=True)
        hot = lane == idx
        vals.append(m)
        hots.append(hot)
        eid = jnp.where(lane == float(k), idx, eid)
        x = jnp.where(hot, -jnp.inf, x)
    es = [jnp.exp(v - vals[0]) for v in vals]
    den = es[0]
    for e in es[1:]:
        den = den + e
    gate = jnp.zeros(x.shape, F32)
    for k in range(TOP_K):
        gate = jnp.where(lane == float(k), es[k] / den, gate)

    cnt_tok = hots[0].astype(F32)
    for hot in hots[1:]:
        cnt_tok = cnt_tok + hot.astype(F32)
    earlier = lax.broadcasted_iota(jnp.int32, (tm, tm), 1) < lax.broadcasted_iota(jnp.int32, (tm, tm), 0)
    before = run_scr[...] + jnp.dot(earlier.astype(BF16), cnt_tok.astype(BF16), preferred_element_type=F32)
    rank = jnp.zeros(x.shape, F32)
    for k in range(TOP_K):
        rank = jnp.where(lane == float(k), jnp.sum(jnp.where(hots[k], before, 0.0), axis=-1, keepdims=True), rank)
    run_scr[...] = run_scr[...] + jnp.sum(cnt_tok, axis=0, keepdims=True)

    eid_ref[...] = eid.astype(jnp.int32)
    gate_ref[...] = gate
    rank_ref[...] = rank.astype(jnp.int32)
    cnt_ref[...] = run_scr[...].astype(jnp.int32)


def _route(logits):
    n = logits.shape[0]
    tm = ROW_TILE
    row = pl.BlockSpec((tm, EXPERT_LANES), lambda i: (i, 0))
    eid, gate, rank, cnt = pl.pallas_call(
        _route_kernel,
        grid=(n // tm,),
        in_specs=[row],
        out_specs=[row, row, row, _const_spec((1, EXPERT_LANES))],
        out_shape=[jax.ShapeDtypeStruct((n, EXPERT_LANES), jnp.int32), jax.ShapeDtypeStruct((n, EXPERT_LANES), F32),
                   jax.ShapeDtypeStruct((n, EXPERT_LANES), jnp.int32),
                   jax.ShapeDtypeStruct((1, EXPERT_LANES), jnp.int32)],
        scratch_shapes=[pltpu.VMEM((1, EXPERT_LANES), F32)],
        compiler_params=_cparams(("arbitrary",), 32),
        name="route",
    )(logits)

    counts = cnt[0, :N_EXPERTS]
    padded = (counts + MOE_ROWS - 1) // MOE_ROWS * MOE_ROWS
    p_end = jnp.cumsum(padded)
    p_start = p_end - padded
    experts = jnp.arange(N_EXPERTS, dtype=jnp.int32)
    start_of = jnp.sum(jnp.where(eid[:, :TOP_K, None] == experts, p_start, 0), axis=-1)
    dest_flat = (start_of + rank[:, :TOP_K]).reshape(n * TOP_K).astype(jnp.int32)
    n_blocks = -(-(n * TOP_K + N_EXPERTS * (MOE_ROWS - 1)) // MOE_ROWS)
    block_start = jnp.arange(n_blocks, dtype=jnp.int32) * MOE_ROWS
    block_e = jnp.minimum(jnp.sum(p_end[None, :] <= block_start[:, None], axis=1), N_EXPERTS - 1).astype(jnp.int32)
    n_valid = (p_end[-1] // MOE_ROWS).astype(jnp.int32).reshape(1)
    block_e = jnp.where(jnp.arange(n_blocks) < n_valid[0], block_e, block_e[n_valid[0] - 1])
    zero_rows = jnp.concatenate([jnp.where(padded > 0, p_end - MOE_ROWS, -1), n_valid]).astype(jnp.int32)
    return gate, dest_flat, block_e, n_valid, zero_rows, n_blocks * MOE_ROWS


def _in_splits(dt, v_layout, with_keys):
    off = np.cumsum([0, D_MODEL, Q_WIDTH, K_WIDTH, V_WIDTH, D_MODEL, N_BRANCH * D_MODEL])
    q_scale = HEAD_DIM ** -0.5 * math.log2(math.e)
    keys = ((int(off[2]), K_WIDTH, 1.0, ((F32, PLAIN),)),) if with_keys else ()
    return ((int(off[0]), D_MODEL, 1.0, ((F32, PLAIN),)),
            (int(off[1]), Q_WIDTH, q_scale, ((dt, PLAIN),)),
            *keys,
            (int(off[3]), V_WIDTH, 1.0, ((F32, v_layout), (BF16, WITH_ONES))),
            (int(off[4]), D_MODEL, X_DIM ** -0.5, ((dt, PLAIN),)),
            (int(off[5]), N_BRANCH * D_MODEL, 1.0, ((F32, PLAIN),)))


def kernel(x_prompt, x_sample, mem_prompt, cache_k, cache_v, cache_mem_k, cache_mem_v, state_pool, page_table, g_mix, w_in, w_pool_group, pool_scale, lambda_q1, lambda_k1, lambda_q2, lambda_k2, g_subln, g_mem, w_mem_kv, w_pool_proj, w_diff_proj, w_cross_proj, w_o, g_ffn, w_router, b_router, w_up, b_up, w_down, b_down, g_final):
    b, s, d = x_prompt.shape
    db, t, _ = x_sample.shape
    depth = w_in.shape[0]
    assert depth == 1, "the step is written for the single-layer trunk this problem states"
    n_phys, page = cache_k.shape[1], cache_k.shape[2]
    n_mem = mem_prompt.shape[1]
    n_p, n_s = b * s, db * t

    cache_kt = cache_k.transpose(0, 1, 3, 4, 5, 2).reshape(depth * n_phys * K_WIDTH, page)
    cache_v2 = cache_v.reshape(depth * n_phys * page * N_KV_HEADS, V_DIM)
    cmk = _memory_rows(cache_mem_k)
    cmv = _memory_rows(cache_mem_v)

    xp = x_prompt.reshape(n_p, d)
    xs = x_sample.reshape(n_s, d)
    outs = {name: [] for name in ("kp", "vp", "mk", "mv", "hp", "ks", "vs", "hs")}

    for l in range(depth):
        lam_init = 0.8 - 0.6 * math.exp(-0.3 * l)
        lam_vecs = jnp.stack([lambda_q1[l], lambda_k1[l], lambda_q2[l], lambda_k2[l]]).astype(F32)
        w_in_bf = w_in[l].astype(BF16)
        wr = jnp.zeros((d, EXPERT_LANES), BF16).at[:, :N_EXPERTS].set(w_router[l].astype(BF16))
        br = jnp.full((1, EXPERT_LANES), MASK_VALUE, F32).at[0, :N_EXPERTS].set(b_router[l].astype(F32))
        wts = dict(wg=w_pool_group[l].astype(BF16), pool_scale=pool_scale[l].reshape(1, d).astype(F32),
                   wpp=w_pool_proj[l].astype(BF16), wdp=w_diff_proj[l].astype(BF16),
                   wcp=w_cross_proj[l].astype(BF16), wo=w_o[l].astype(BF16),
                   g_ffn=g_ffn[l].reshape(1, d).astype(F32), wr=wr, br=br)

        mk, mv = _rms_proj(mem_prompt.reshape(b * n_mem, d), g_mem[l], w_mem_kv[l].astype(BF16),
                           ((0, d, 1.0, ((F32, MEMORY_ROWS),)), (d, d, 1.0, ((F32, MEMORY_ROWS),))))

        k_off = D_MODEL + Q_WIDTH
        wk_t = w_in[l][:, k_off:k_off + K_WIDTH].T.astype(BF16)
        u_p, q_p, v_p, v_pb, qx_p, gl_p, kt_p, kt_pb = _rms_proj(
            xp, g_mix[l], w_in_bf, _in_splits(BF16, VALUE_ROWS, with_keys=False), wt_bf=wk_t,
            transposed=(F32, BF16), tokens_per_seq=s)
        pooled_p = _pool_prompt(u_p.reshape(b, s, d)).reshape(n_p, d)
        adiff_p = _attn_prompt(q_p.reshape(b, s, Q_WIDTH), kt_pb, v_pb.reshape(b, s, 2 * V_WIDTH), lam_vecs,
                               g_subln[l], lam_init).reshape(n_p, d)
        across_p = _cross(qx_p.reshape(b, s, d), mk, mv, n_mem, 0, BF16).reshape(n_p, d)

        u_s, q_s, k_s, v_s, _, qx_s, gl_s = _rms_proj(xs, g_mix[l], w_in_bf, _in_splits(F32, PLAIN, with_keys=True))
        pooled_s = _pool_sample(state_pool[l], u_s.reshape(db, t, d)).reshape(n_s, d)
        adiff_s = _attn_sample(q_s.reshape(db, t, Q_WIDTH), k_s.reshape(db, t, K_WIDTH), v_s.reshape(db, t, V_WIDTH),
                               cache_kt, cache_v2, page, page_table + l * n_phys, lam_vecs, g_subln[l],
                               lam_init).reshape(n_s, d)
        across_s = _cross(qx_s.reshape(db, t, d), cmk, cmv, n_mem, l * db, F32).reshape(n_s, d)

        x1, h2, logits = _merge((pooled_p, adiff_p, across_p, gl_p, xp), (pooled_s, adiff_s, across_s, gl_s, xs), wts)
        gate, dest_flat, block_e, n_valid, zero_rows, n_slots = _route(logits)
        out_buf = _experts(_dispatch(h2, dest_flat, zero_rows, n_slots), block_e, n_valid, w_up[l],
                           b_up[l].reshape(N_EXPERTS, 1, 2 * D_FF), w_down[l], b_down[l].reshape(N_EXPERTS, 1, d))
        xp, xs = _combine(x1, out_buf, dest_flat, gate, g_final, n_p)

        outs["kp"].append(kt_p.reshape(b, N_KV_HEADS, 2, HEAD_DIM, s).transpose(0, 4, 1, 2, 3))
        outs["vp"].append(v_p.reshape(b, s, N_KV_HEADS, V_DIM))
        outs["mk"].append(_memory_from_rows(mk, 1, b, n_mem)[0])
        outs["mv"].append(_memory_from_rows(mv, 1, b, n_mem)[0])
        outs["hp"].append(u_p.reshape(b, s, d)[:, -POOL_STATE:].astype(state_pool.dtype))
        outs["ks"].append(k_s.reshape(db, t, N_KV_HEADS, 2, HEAD_DIM))
        outs["vs"].append(v_s.reshape(db, t, N_KV_HEADS, V_DIM))
        outs["hs"].append(jnp.concatenate([state_pool[l], u_s.reshape(db, t, d).astype(state_pool.dtype)],
                                          axis=1)[:, -POOL_STATE:])

    return (xp.reshape(b, s, d), xs.reshape(db, t, d), jnp.stack(outs["kp"]), jnp.stack(outs["vp"]),
            jnp.stack(outs["mk"]), jnp.stack(outs["mv"]), jnp.stack(outs["hp"]), jnp.stack(outs["ks"]),
            jnp.stack(outs["vs"]), jnp.stack(outs["hs"]))
```

```python
import functools
import math

import numpy as np
import jax
import jax.numpy as jnp
from jax import lax
from jax.experimental import pallas as pl
from jax.experimental.pallas import tpu as pltpu

F32 = jnp.float32
BF16 = jnp.bfloat16

D_MODEL = 1024
N_HEADS = 8
HEAD_DIM = 64
N_KV_HEADS = 4
GQA_GROUP = N_HEADS // N_KV_HEADS
V_DIM = 2 * HEAD_DIM
POOL_WINDOWS = (2, 4, 8, 16)
POOL_GROUPS = 4
POOL_GROUP_DIM = D_MODEL // POOL_GROUPS
POOL_STATE = 15
POOL_HALO = 16
X_HEADS = 4
X_DIM = D_MODEL // X_HEADS
N_BRANCH = 3
N_EXPERTS = 32
TOP_K = 4
D_FF = D_MODEL
SWIGLU_LIMIT = 7.0
SWIGLU_ALPHA = 1.702
RMS_EPS = 1e-6
Q_WIDTH = N_HEADS * 2 * HEAD_DIM
K_WIDTH = N_KV_HEADS * 2 * HEAD_DIM
V_WIDTH = N_KV_HEADS * V_DIM
KV_BLOCK = 2 * HEAD_DIM
MASK_VALUE = -1e30

LANES = 128
EXPERT_LANES = LANES
MIB = 2 ** 20

ROW_TILE = 256
ATTN_TQ = 512
ATTN_TK = 512
ATTN_ITEM_ROWS = 512
PAGES_PER_STEP = 16
CROSS_TQ = 1024
MOE_ROWS = 512
DISPATCH_TILE = 1024
COMBINE_TILE = 512
POOL_SEQ_BLOCK = 8


def _cparams(semantics, vmem_mib):
    return pltpu.CompilerParams(dimension_semantics=semantics, vmem_limit_bytes=vmem_mib * MIB)


def _const_spec(shape):
    zeros = (0,) * len(shape)
    return pl.BlockSpec(shape, lambda *_: zeros)


def _rms(xf, g):
    return xf * lax.rsqrt(jnp.mean(xf * xf, axis=-1, keepdims=True) + RMS_EPS) * g


TOKEN_TILE_ROWS = D_MODEL // LANES


def _store_token_tiles(ref, x):
    n = x.shape[0]
    for s in range(TOKEN_TILE_ROWS):
        ref[pl.ds(s, n, stride=TOKEN_TILE_ROWS), :] = x[:, s * LANES:(s + 1) * LANES]


def _load_token_tiles(ref, n, lead=()):
    return [ref[lead + (pl.ds(s, n, stride=TOKEN_TILE_ROWS), slice(None))] for s in range(TOKEN_TILE_ROWS)]


PLAIN = ("plain",)
WITH_ONES = ("with_ones",)


def _chunk_rows(positions):
    return ("chunk_rows", tuple(positions))


VALUE_ROWS = _chunk_rows(range(N_KV_HEADS))
MEMORY_ROWS = _chunk_rows([(q % 2) * X_HEADS + q // 2 for q in range(X_HEADS * 2)])


POOLED = ("pooled",)


POOL_HISTORY = 128


def _pool_rows(ue, tok, o_ref, first_pos):
    tm = tok.shape[0]
    rows = ue.shape[0]
    own = lax.broadcasted_iota(jnp.int32, (tm, rows), 0) + POOL_HISTORY
    col = lax.broadcasted_iota(jnp.int32, (tm, rows), 1)
    e = ue[...]
    head = e.astype(BF16)
    rest = (e - head.astype(F32)).astype(BF16)
    pos = first_pos + lax.broadcasted_iota(jnp.int32, (tm, 1), 0)
    for g, w in enumerate(POOL_WINDOWS):
        c = slice(g * POOL_GROUP_DIM, (g + 1) * POOL_GROUP_DIM)
        band = jnp.where(jnp.logical_and(col <= own, col > own - w), 1.0, 0.0).astype(BF16)
        total = (jnp.dot(band, head[:, c], preferred_element_type=F32)
                 + jnp.dot(band, rest[:, c], preferred_element_type=F32))
        cnt = jnp.minimum(pos + 1, w).astype(F32)
        o_ref[:, c] = (total / cnt - tok[:, c]).astype(o_ref.dtype)


def _rms_proj_kernel(x_ref, g_ref, w_ref, *rest, splits, transposed, tiles_per_seq):
    hb = _rms(x_ref[...].astype(F32), g_ref[...]).astype(BF16)
    if transposed:
        wt_ref, rest = rest[0], rest[1:]
    out_refs = rest
    tm = x_ref.shape[0]
    oi = 0
    for c0, width, scale, copies in splits:
        z = jnp.dot(hb, w_ref[:, c0:c0 + width], preferred_element_type=F32)
        if scale != 1.0:
            z = z * scale
        for dt, layout in copies:
            o_ref = out_refs[oi]
            oi += 1
            if layout == POOLED:
                tail_ref, ue = out_refs[oi], out_refs[-1]
                oi += 1
                tile_in_seq = pl.program_id(0) % tiles_per_seq
                @pl.when(tile_in_seq == 0)
                def _sequence_start():
                    ue[0:POOL_HISTORY, :] = jnp.zeros((POOL_HISTORY, ue.shape[1]), F32)

                @pl.when(tile_in_seq != 0)
                def _carry_history():
                    ue[0:POOL_HISTORY, :] = ue[tm:tm + POOL_HISTORY, :]

                ue[POOL_HISTORY:, :] = z
                _pool_rows(ue, z, o_ref, tile_in_seq * tm)
                tail_ref[...] = z[tm - POOL_HALO:, :]
            elif layout == WITH_ONES:
                for h in range(width // V_DIM):
                    o_ref[:, 2 * h * V_DIM:(2 * h + 1) * V_DIM] = z[:, h * V_DIM:(h + 1) * V_DIM].astype(dt)
                    o_ref[:, (2 * h + 1) * V_DIM:(2 * h + 2) * V_DIM] = jnp.ones((z.shape[0], V_DIM), dt)
            elif layout[0] == "chunk_rows":
                positions = layout[1]
                for q, pos in enumerate(positions):
                    o_ref[pl.ds(pos, z.shape[0], stride=len(positions)), :] = z[:, q * LANES:(q + 1) * LANES].astype(dt)
            else:
                o_ref[...] = z.astype(dt)
    if transposed:
        zt = lax.dot_general(wt_ref[...], hb, (((1,), (1,)), ((), ())), preferred_element_type=F32)
        for dt in transposed:
            out_refs[oi][...] = zt.astype(dt)
            oi += 1


def _rms_proj(x2d, g, w_bf, splits, wt_bf=None, transposed=(), tokens_per_seq=None):
    n, d = x2d.shape
    tm = ROW_TILE
    assert n % tm == 0
    tiles_per_seq = None
    if tokens_per_seq is not None:
        assert tokens_per_seq % tm == 0 and n % tokens_per_seq == 0
        tiles_per_seq = tokens_per_seq // tm
    out_shape, out_specs, scratch = [], [], []
    for _, width, _, copies in splits:
        for dt, layout in copies:
            if layout == POOLED:
                out_shape += [jax.ShapeDtypeStruct((n, width), dt),
                              jax.ShapeDtypeStruct((n // tokens_per_seq, POOL_HALO, width), F32)]
                out_specs += [pl.BlockSpec((tm, width), lambda i: (i, 0)),
                              pl.BlockSpec((None, POOL_HALO, width), lambda i: (i // tiles_per_seq, 0, 0))]
                assert tm >= POOL_HISTORY >= POOL_WINDOWS[-1] - 1
                scratch.append(pltpu.VMEM((POOL_HISTORY + tm, width), F32))
                continue
            if layout[0] == "chunk_rows":
                n_chunks = len(layout[1])
                assert width == n_chunks * LANES
                shape, block = (n * n_chunks, LANES), (tm * n_chunks, LANES)
            else:
                w_out = 2 * width if layout == WITH_ONES else width
                shape, block = (n, w_out), (tm, w_out)
            out_shape.append(jax.ShapeDtypeStruct(shape, dt))
            out_specs.append(pl.BlockSpec(block, lambda i: (i, 0)))
    operands = [x2d, g.reshape(1, d), w_bf]
    in_specs = [pl.BlockSpec((tm, d), lambda i: (i, 0)), _const_spec((1, d)), _const_spec(w_bf.shape)]
    if transposed:
        feats = wt_bf.shape[0]
        operands.append(wt_bf)
        in_specs.append(_const_spec(wt_bf.shape))
        for dt in transposed:
            out_shape.append(jax.ShapeDtypeStruct((n // tokens_per_seq, feats, tokens_per_seq), dt))
            out_specs.append(pl.BlockSpec((None, feats, tm), lambda i: (i // tiles_per_seq, 0, i % tiles_per_seq)))
    return pl.pallas_call(
        functools.partial(_rms_proj_kernel, splits=splits, transposed=tuple(transposed),
                          tiles_per_seq=tiles_per_seq),
        grid=(n // tm,),
        in_specs=in_specs,
        out_specs=out_specs,
        out_shape=out_shape,
        scratch_shapes=scratch,
        compiler_params=_cparams(("arbitrary",), 56),
        name="rms_proj",
    )(*operands)


def _pool_sample_kernel(ue_ref, o_ref, *, n_new):
    lo = ue_ref.shape[1] - n_new
    for s in range(ue_ref.shape[0]):
        for g, w in enumerate(POOL_WINDOWS):
            c = slice(g * POOL_GROUP_DIM, (g + 1) * POOL_GROUP_DIM)
            tok = ue_ref[s, lo:lo + n_new, c]
            acc = tok
            for j in range(1, w):
                acc = acc + ue_ref[s, lo - j:lo - j + n_new, c]
            o_ref[s, :, c] = acc / float(w) - tok


def _pool_sample(hist, u3):
    db, t, d = u3.shape
    assert hist.shape[1] >= POOL_WINDOWS[-1] - 1
    pad = (-(hist.shape[1] + t)) % 8
    ue = jnp.concatenate([jnp.zeros((db, pad, d), F32), hist.astype(F32), u3], axis=1)
    rows = ue.shape[1]
    sb = POOL_SEQ_BLOCK
    return pl.pallas_call(
        functools.partial(_pool_sample_kernel, n_new=t),
        grid=(db // sb,),
        in_specs=[pl.BlockSpec((sb, rows, d), lambda i: (i, 0, 0))],
        out_specs=pl.BlockSpec((sb, t, d), lambda i: (i, 0, 0)),
        out_shape=jax.ShapeDtypeStruct((db, t, d), F32),
        compiler_params=_cparams(("parallel",), 32),
        name="pool_sample",
    )(ue)


def _lambda_value(lam_ref, lam_init):
    a = jnp.sum(lam_ref[0:1, :] * lam_ref[1:2, :], axis=-1, keepdims=True)
    b = jnp.sum(lam_ref[2:3, :] * lam_ref[3:4, :], axis=-1, keepdims=True)
    return jnp.exp(a) - jnp.exp(b) + lam_init


def _subln(o, gsub, lam_init):
    return _rms(o, gsub) * (1.0 - lam_init)


def _softmax_update(s, m_prev):
    m_new = jnp.maximum(m_prev, jnp.max(s, axis=-1, keepdims=True))
    alpha = jnp.exp2(m_prev - m_new)
    p = jnp.exp2(s - jnp.tile(m_new, (1, s.shape[1] // LANES)))
    return p, m_new, alpha


def _attn_prompt_kernel(qi_ref, ki_ref, last_ref, q_ref, k_ref, v_ref, lam_ref, gsub_ref, o_ref,
                        qpad, acc, m_scr, *, tq, tk, lam_init):
    step = pl.program_id(1)
    qi = qi_ref[step]
    ki = ki_ref[step]
    n_blocks = 2 * GQA_GROUP

    @pl.when(ki == 0)
    def _init():
        lane = lax.broadcasted_iota(jnp.int32, (tq, KV_BLOCK), 1)
        for h in range(N_KV_HEADS):
            for g in range(GQA_GROUP):
                c0 = (h * GQA_GROUP + g) * KV_BLOCK
                blk = q_ref[:, c0:c0 + KV_BLOCK]
                qpad[h, g * tq:(g + 1) * tq, :] = jnp.where(lane < HEAD_DIM, blk, jnp.zeros_like(blk))
                qpad[h, (GQA_GROUP + g) * tq:(GQA_GROUP + g + 1) * tq, :] = jnp.where(lane >= HEAD_DIM, blk,
                                                                                     jnp.zeros_like(blk))
        acc[...] = jnp.zeros_like(acc)
        m_scr[...] = jnp.full_like(m_scr, MASK_VALUE)

    items = [(h, r0) for h in range(N_KV_HEADS) for r0 in range(0, n_blocks * tq, ATTN_ITEM_ROWS)]

    def item_rows(r0):
        return slice(r0, r0 + ATTN_ITEM_ROWS)

    def scores(item):
        h, r0 = item
        return jnp.dot(qpad[h, item_rows(r0), :], k_ref[h * KV_BLOCK:(h + 1) * KV_BLOCK, :],
                       preferred_element_type=F32)

    def body(masked):
        if masked:
            kpos = ki * tk + lax.broadcasted_iota(jnp.int32, (ATTN_ITEM_ROWS, tk), 1)
            q_in_item = lax.broadcasted_iota(jnp.int32, (ATTN_ITEM_ROWS, tk), 0)
        s_next = scores(items[0])
        for n, (h, r0) in enumerate(items):
            s = s_next
            if n + 1 < len(items):
                s_next = scores(items[n + 1])
            if masked:
                s = jnp.where(kpos <= qi * tq + r0 % tq + q_in_item, s, MASK_VALUE)
            rows = item_rows(r0)
            p, m_new, alpha = _softmax_update(s, m_scr[h, rows, :])
            m_scr[h, rows, :] = m_new
            pv = jnp.dot(p.astype(BF16), v_ref[:, h * 2 * V_DIM:(h + 1) * 2 * V_DIM], preferred_element_type=F32)
            acc[h, rows, :] = jnp.tile(alpha, (1, 2)) * acc[h, rows, :] + pv

    needs_mask = (ki + 1) * tk - 1 > qi * tq

    @pl.when(needs_mask)
    def _masked():
        body(True)

    @pl.when(jnp.logical_not(needs_mask))
    def _unmasked():
        body(False)

    @pl.when(last_ref[step] == 1)
    def _finish():
        lam = _lambda_value(lam_ref, lam_init)
        gsub = gsub_ref[...]
        for h in range(N_KV_HEADS):
            for g in range(GQA_GROUP):
                rows1 = slice(g * tq, (g + 1) * tq)
                rows2 = slice((GQA_GROUP + g) * tq, (GQA_GROUP + g + 1) * tq)
                o1 = acc[h, rows1, 0:V_DIM] / acc[h, rows1, V_DIM:2 * V_DIM]
                o2 = acc[h, rows2, 0:V_DIM] / acc[h, rows2, V_DIM:2 * V_DIM]
                c0 = (h * GQA_GROUP + g) * V_DIM
                o_ref[:, c0:c0 + V_DIM] = _subln(o1 - lam * o2, gsub, lam_init).astype(o_ref.dtype)


def _attn_prompt(q, k, v, lam_vecs, gsub, lam_init):
    b, s, _ = q.shape
    tq, tk = ATTN_TQ, ATTN_TK
    assert s % tq == 0 and s % tk == 0
    qi_l, ki_l, last_l = [], [], []
    for qi in range(s // tq):
        n_k = min(((qi + 1) * tq - 1) // tk + 1, s // tk)
        for ki in range(n_k):
            qi_l.append(qi)
            ki_l.append(ki)
            last_l.append(1 if ki == n_k - 1 else 0)
    tabs = [jnp.asarray(np.array(t, np.int32)) for t in (qi_l, ki_l, last_l)]
    rows = 2 * GQA_GROUP * tq
    assert tq % ATTN_ITEM_ROWS == 0
    grid_spec = pltpu.PrefetchScalarGridSpec(
        num_scalar_prefetch=3,
        grid=(b, len(qi_l)),
        in_specs=[pl.BlockSpec((None, tq, Q_WIDTH), lambda bi, st, qt, kt, lt: (bi, qt[st], 0)),
                  pl.BlockSpec((None, K_WIDTH, tk), lambda bi, st, qt, kt, lt: (bi, 0, kt[st])),
                  pl.BlockSpec((None, tk, 2 * V_WIDTH), lambda bi, st, qt, kt, lt: (bi, kt[st], 0)),
                  _const_spec((4, HEAD_DIM)), _const_spec((1, V_DIM))],
        out_specs=pl.BlockSpec((None, tq, N_HEADS * V_DIM), lambda bi, st, qt, kt, lt: (bi, qt[st], 0)),
        scratch_shapes=[pltpu.VMEM((N_KV_HEADS, rows, KV_BLOCK), BF16),
                        pltpu.VMEM((N_KV_HEADS, rows, 2 * V_DIM), F32),
                        pltpu.VMEM((N_KV_HEADS, rows, LANES), F32)])
    return pl.pallas_call(
        functools.partial(_attn_prompt_kernel, tq=tq, tk=tk, lam_init=lam_init),
        grid_spec=grid_spec,
        out_shape=jax.ShapeDtypeStruct((b, s, N_HEADS * V_DIM), BF16),
        compiler_params=_cparams(("parallel", "arbitrary"), 56),
        name="attn_prompt",
    )(*tabs, q, k, v, lam_vecs, gsub.reshape(1, V_DIM))


SAMPLE_ROWS = N_KV_HEADS * 2 * GQA_GROUP


def _attn_sample_kernel(pt_ref, wq_ref, kn_ref, vn_ref, lam_ref, gsub_ref, kt_hbm, v_hbm, o_ref,
                        kbuf, vbuf, acc, m_scr, l_scr, sem_k, sem_v, *, n_pages, t_new, page, lam_init):
    seq = pl.program_id(0)
    step = pl.program_id(1)
    n_steps = pl.num_programs(1)
    rows = SAMPLE_ROWS * t_new
    k_rows = kbuf.shape[1] // n_pages
    v_rows = vbuf.shape[1] // n_pages
    flat = seq * n_steps + step
    slot = flat % 2

    def page_copies(seq_i, step_i, slot_i):
        copies = []
        for j in range(n_pages):
            pid = pt_ref[seq_i, step_i * n_pages + j]
            copies.append(pltpu.make_async_copy(kt_hbm.at[pl.ds(pl.multiple_of(pid * k_rows, k_rows), k_rows), :],
                                                kbuf.at[slot_i, pl.ds(j * k_rows, k_rows), :], sem_k.at[slot_i]))
            copies.append(pltpu.make_async_copy(v_hbm.at[pl.ds(pl.multiple_of(pid * v_rows, v_rows), v_rows), :],
                                                vbuf.at[slot_i, pl.ds(j * v_rows, v_rows), :], sem_v.at[slot_i]))
        return copies

    @pl.when(flat == 0)
    def _first_fetch():
        for c in page_copies(seq, step, slot):
            c.start()

    @pl.when(flat + 1 < pl.num_programs(0) * n_steps)
    def _prefetch():
        wrap = step + 1 == n_steps
        for c in page_copies(jnp.where(wrap, seq + 1, seq), jnp.where(wrap, 0, step + 1), 1 - slot):
            c.start()

    @pl.when(step == 0)
    def _init():
        acc[...] = jnp.zeros_like(acc)
        m_scr[...] = jnp.full_like(m_scr, MASK_VALUE)
        l_scr[...] = jnp.zeros_like(l_scr)

    pltpu.make_async_copy(kt_hbm.at[pl.ds(0, n_pages * k_rows), :], kbuf.at[slot], sem_k.at[slot]).wait()
    pltpu.make_async_copy(v_hbm.at[pl.ds(0, n_pages * v_rows), :], vbuf.at[slot], sem_v.at[slot]).wait()

    wq = wq_ref[...]

    def update(s, v_list):
        p, m_new, alpha = _softmax_update(s, m_scr[...])
        m_scr[...] = m_new
        l_scr[...] = alpha * l_scr[...] + jnp.sum(p, axis=-1, keepdims=True)
        pb = p.astype(BF16)
        pv = None
        for j, vj in enumerate(v_list):
            c = vj.shape[0]
            term = jnp.dot(pb[:, j * c:(j + 1) * c], vj, preferred_element_type=F32)
            pv = term if pv is None else pv + term
        acc[...] = jnp.tile(alpha, (1, N_KV_HEADS)) * acc[...] + pv

    def page_keys(j):
        return kbuf[slot, pl.ds(j * k_rows, k_rows), :].astype(BF16)

    def page_values(j):
        heads = [vbuf[slot, pl.ds(j * v_rows + h, page, stride=N_KV_HEADS), :] for h in range(N_KV_HEADS)]
        return jnp.concatenate(heads, axis=1).astype(BF16)

    s_list = [jnp.dot(wq, page_keys(j), preferred_element_type=F32) for j in range(n_pages)]
    update(jnp.concatenate(s_list, axis=1), [page_values(j) for j in range(n_pages)])

    @pl.when(step == pl.num_programs(1) - 1)
    def _finish():
        pad = jnp.zeros((LANES - t_new, kn_ref.shape[1]), F32)
        kn = jnp.concatenate([kn_ref[...], pad], axis=0).astype(BF16)
        vn = jnp.concatenate([vn_ref[...], pad], axis=0).astype(BF16)
        s_new = lax.dot_general(wq, kn, (((1,), (1,)), ((), ())), preferred_element_type=F32)
        t_of_row = lax.broadcasted_iota(jnp.int32, (rows, LANES), 0) % t_new
        key = lax.broadcasted_iota(jnp.int32, (rows, LANES), 1)
        update(jnp.where(key <= t_of_row, s_new, MASK_VALUE), [vn])

        lam = _lambda_value(lam_ref, lam_init)
        gsub = gsub_ref[...]
        for h in range(N_KV_HEADS):
            for g in range(GQA_GROUP):
                r1 = (h * 2 * GQA_GROUP + g) * t_new
                r2 = (h * 2 * GQA_GROUP + GQA_GROUP + g) * t_new
                cols = slice(h * V_DIM, (h + 1) * V_DIM)
                o1 = acc[r1:r1 + t_new, cols] / l_scr[r1:r1 + t_new, :]
                o2 = acc[r2:r2 + t_new, cols] / l_scr[r2:r2 + t_new, :]
                c0 = (h * GQA_GROUP + g) * V_DIM
                o_ref[:, c0:c0 + V_DIM] = _subln(o1 - lam * o2, gsub, lam_init).astype(o_ref.dtype)


def _sample_query_rows(q3):
    db, t, _ = q3.shape
    q6 = q3.reshape(db, t, N_KV_HEADS, GQA_GROUP, 2, HEAD_DIM).transpose(0, 2, 4, 3, 1, 5)
    eye = jnp.eye(N_KV_HEADS * 2, dtype=q3.dtype).reshape(N_KV_HEADS, 2, N_KV_HEADS, 2)
    wq = jnp.einsum("bhmgtd,hmHM->bhmgtHMd", q6, eye)
    return wq.reshape(db, SAMPLE_ROWS * t, K_WIDTH).astype(BF16)


def _attn_sample(q3, k_new, v_new, cache_kt, cache_v, page, page_ids, lam_vecs, gsub, lam_init):
    db, t, _ = q3.shape
    n_pages_total = page_ids.shape[1]
    g = PAGES_PER_STEP
    rows = SAMPLE_ROWS * t
    assert n_pages_total % g == 0 and rows % 8 == 0 and t <= LANES and page % LANES == 0
    wq = _sample_query_rows(q3)

    assert page == LANES and cache_kt.shape[1] == page and cache_v.shape[1] == V_DIM
    grid_spec = pltpu.PrefetchScalarGridSpec(
        num_scalar_prefetch=1,
        grid=(db, n_pages_total // g),
        in_specs=[pl.BlockSpec((None, rows, K_WIDTH), lambda bi, st, pt: (bi, 0, 0)),
                  pl.BlockSpec((None, t, K_WIDTH), lambda bi, st, pt: (bi, 0, 0)),
                  pl.BlockSpec((None, t, V_WIDTH), lambda bi, st, pt: (bi, 0, 0)),
                  _const_spec((4, HEAD_DIM)), _const_spec((1, V_DIM)),
                  pl.BlockSpec(memory_space=pl.ANY), pl.BlockSpec(memory_space=pl.ANY)],
        out_specs=pl.BlockSpec((None, t, N_HEADS * V_DIM), lambda bi, st, pt: (bi, 0, 0)),
        scratch_shapes=[pltpu.VMEM((2, g * K_WIDTH, page), F32),
                        pltpu.VMEM((2, g * page * N_KV_HEADS, V_DIM), F32),
                        pltpu.VMEM((rows, V_WIDTH), F32),
                        pltpu.VMEM((rows, LANES), F32),
                        pltpu.VMEM((rows, LANES), F32),
                        pltpu.SemaphoreType.DMA((2,)), pltpu.SemaphoreType.DMA((2,))])
    return pl.pallas_call(
        functools.partial(_attn_sample_kernel, n_pages=g, t_new=t, page=page, lam_init=lam_init),
        grid_spec=grid_spec,
        out_shape=jax.ShapeDtypeStruct((db, t, N_HEADS * V_DIM), F32),
        compiler_params=_cparams(("arbitrary", "arbitrary"), 40),
        name="attn_sample",
    )(page_ids, wq, k_new, v_new, lam_vecs, gsub.reshape(1, V_DIM), cache_kt, cache_v)


def _cross_kernel(q_ref, mk_ref, mv_ref, o_ref, *, n_mem):
    halves = X_DIM // LANES
    rows_per_token = X_HEADS * halves

    def head(ref, h):
        parts = [ref[pl.ds(c * X_HEADS + h, n_mem, stride=rows_per_token), :] for c in range(halves)]
        return jnp.concatenate(parts, axis=1).astype(BF16)

    for h in range(X_HEADS):
        c = slice(h * X_DIM, (h + 1) * X_DIM)
        q = q_ref[:, c].astype(BF16)
        s = lax.dot_general(q, head(mk_ref, h), (((1,), (1,)), ((), ())), preferred_element_type=F32)
        p = jnp.exp(s - jnp.max(s, axis=-1, keepdims=True))
        l = jnp.sum(p, axis=-1, keepdims=True)
        o = jnp.dot(p.astype(BF16), head(mv_ref, h), preferred_element_type=F32)
        o_ref[:, c] = (o / l).astype(o_ref.dtype)


def _memory_rows(mem5):
    l, b, n_mem, heads, xd = mem5.shape
    m6 = mem5.reshape(l, b, n_mem, heads, xd // LANES, LANES).transpose(0, 1, 2, 4, 3, 5)
    return m6.reshape(l * b * n_mem * heads * (xd // LANES), LANES)


def _memory_from_rows(rows, l, b, n_mem):
    halves = X_DIM // LANES
    return rows.reshape(l, b, n_mem, halves, X_HEADS, LANES).transpose(0, 1, 2, 4, 3, 5).reshape(
        l, b, n_mem, X_HEADS, X_DIM)


def _cross(q3, mk, mv, n_mem, mem_offset, out_dtype):
    b, t, d = q3.shape
    tq = min(CROSS_TQ, t)
    mem_rows = n_mem * (d // LANES)
    return pl.pallas_call(
        functools.partial(_cross_kernel, n_mem=n_mem),
        grid=(b, t // tq),
        in_specs=[pl.BlockSpec((None, tq, d), lambda bi, i: (bi, i, 0)),
                  pl.BlockSpec((mem_rows, LANES), lambda bi, i: (bi + mem_offset, 0)),
                  pl.BlockSpec((mem_rows, LANES), lambda bi, i: (bi + mem_offset, 0))],
        out_specs=pl.BlockSpec((None, tq, d), lambda bi, i: (bi, i, 0)),
        out_shape=jax.ShapeDtypeStruct((b, t, d), out_dtype),
        compiler_params=_cparams(("parallel", "parallel"), 32),
        name="cross_attn",
    )(q3, mk, mv)


N_MERGE_INPUTS = 5


def _merge_kernel(*refs, first_tiles):
    first, second = refs[:N_MERGE_INPUTS], refs[N_MERGE_INPUTS:2 * N_MERGE_INPUTS]
    (wg_ref, ps_ref, wpp_ref, wdp_ref, wcp_ref, wo_ref, gffn_ref, wr_ref, br_ref,
     x1_ref, h2_ref, lg_ref) = refs[2 * N_MERGE_INPUTS:]

    def body(pooled_ref, adiff_ref, across_ref, gl_ref, x_ref):
        pooled = pooled_ref[...].astype(BF16)
        ys = [jnp.dot(pooled[:, g * POOL_GROUP_DIM:(g + 1) * POOL_GROUP_DIM], wg_ref[g],
                      preferred_element_type=F32) for g in range(POOL_GROUPS)]
        a_pool = jnp.concatenate(ys, axis=1) * ps_ref[...]
        p_pool = jnp.dot(a_pool.astype(BF16), wpp_ref[...], preferred_element_type=F32)
        p_diff = jnp.dot(adiff_ref[...].astype(BF16), wdp_ref[...], preferred_element_type=F32)
        p_cross = jnp.dot(across_ref[...].astype(BF16), wcp_ref[...], preferred_element_type=F32)
        merged = (jax.nn.sigmoid(gl_ref[:, 0:D_MODEL]) * p_pool
                  + jax.nn.sigmoid(gl_ref[:, D_MODEL:2 * D_MODEL]) * p_diff
                  + jax.nn.sigmoid(gl_ref[:, 2 * D_MODEL:3 * D_MODEL]) * p_cross)
        x1 = x_ref[...] + jnp.dot(merged.astype(BF16), wo_ref[...], preferred_element_type=F32)
        x1_ref[...] = x1
        h2 = _rms(x1, gffn_ref[...])
        _store_token_tiles(h2_ref, h2)
        lg_ref[...] = jnp.dot(h2.astype(BF16), wr_ref[...], preferred_element_type=F32) + br_ref[...]

    i = pl.program_id(0)

    @pl.when(i < first_tiles)
    def _first():
        body(*first)

    @pl.when(i >= first_tiles)
    def _second():
        body(*second)


def _merge(first, second, wts):
    d = D_MODEL
    tm = ROW_TILE
    n1, n2 = first[-1].shape[0], second[-1].shape[0]
    assert n1 % tm == 0 and n2 % tm == 0
    t1 = n1 // tm
    n = n1 + n2
    widths = (d, d, d, N_BRANCH * d, d)
    first_specs = [pl.BlockSpec((tm, w), lambda i: (jnp.minimum(i, t1 - 1), 0)) for w in widths]
    second_specs = [pl.BlockSpec((tm, w), lambda i: (jnp.maximum(i - t1, 0), 0)) for w in widths]
    row = lambda w: pl.BlockSpec((tm, w), lambda i: (i, 0))
    consts = [wts["wg"], wts["pool_scale"], wts["wpp"], wts["wdp"], wts["wcp"], wts["wo"], wts["g_ffn"],
              wts["wr"], wts["br"]]
    return pl.pallas_call(
        functools.partial(_merge_kernel, first_tiles=t1),
        grid=(n // tm,),
        in_specs=first_specs + second_specs + [_const_spec(c.shape) for c in consts],
        out_specs=[row(d), pl.BlockSpec((tm * TOKEN_TILE_ROWS, LANES), lambda i: (i, 0)), row(EXPERT_LANES)],
        out_shape=[jax.ShapeDtypeStruct((n, d), F32), jax.ShapeDtypeStruct((n * TOKEN_TILE_ROWS, LANES), F32),
                   jax.ShapeDtypeStruct((n, EXPERT_LANES), F32)],
        compiler_params=_cparams(("arbitrary",), 56),
        name="merge",
    )(*first, *second, *consts)


def _expert_kernel(be_ref, nv_ref, xs_ref, wu_ref, bu_ref, wd_ref, bd_ref, o_ref, wu_bf, wd_bf):
    i = pl.program_id(0)
    new_expert = jnp.logical_or(i == 0, be_ref[i] != be_ref[jnp.maximum(i - 1, 0)])

    @pl.when(jnp.logical_and(i < nv_ref[0], new_expert))
    def _cast_weights():
        wu_bf[...] = wu_ref[...].astype(BF16)
        wd_bf[...] = wd_ref[...].astype(BF16)

    @pl.when(i < nv_ref[0])
    def _run():
        x = jnp.concatenate(_load_token_tiles(xs_ref, MOE_ROWS), axis=1).astype(BF16)
        up = jnp.dot(x, wu_bf[...], preferred_element_type=F32) + bu_ref[...]
        glu = jnp.minimum(up[:, :D_FF], SWIGLU_LIMIT)
        lin = jnp.clip(up[:, D_FF:], -SWIGLU_LIMIT, SWIGLU_LIMIT)
        act = glu * jax.nn.sigmoid(SWIGLU_ALPHA * glu) * (lin + 1.0)
        _store_token_tiles(o_ref, jnp.dot(act.astype(BF16), wd_bf[...], preferred_element_type=F32) + bd_ref[...])

    @pl.when(i >= nv_ref[0])
    def _skip():
        o_ref[...] = jnp.zeros_like(o_ref)


def _experts(xs, block_e, n_valid, wu, bu, wd, bd):
    d = D_MODEL
    tile_rows = MOE_ROWS * TOKEN_TILE_ROWS
    grid_spec = pltpu.PrefetchScalarGridSpec(
        num_scalar_prefetch=2,
        grid=(xs.shape[0] // tile_rows,),
        in_specs=[pl.BlockSpec((tile_rows, LANES), lambda i, be, nv: (jnp.minimum(i, nv[0] - 1), 0)),
                  pl.BlockSpec((None, d, 2 * D_FF), lambda i, be, nv: (be[i], 0, 0)),
                  pl.BlockSpec((None, 1, 2 * D_FF), lambda i, be, nv: (be[i], 0, 0)),
                  pl.BlockSpec((None, D_FF, d), lambda i, be, nv: (be[i], 0, 0)),
                  pl.BlockSpec((None, 1, d), lambda i, be, nv: (be[i], 0, 0))],
        out_specs=pl.BlockSpec((tile_rows, LANES), lambda i, be, nv: (i, 0)),
        scratch_shapes=[pltpu.VMEM((d, 2 * D_FF), BF16), pltpu.VMEM((D_FF, d), BF16)])
    return pl.pallas_call(
        _expert_kernel,
        grid_spec=grid_spec,
        out_shape=jax.ShapeDtypeStruct(xs.shape, F32),
        compiler_params=_cparams(("arbitrary",), 56),
        name="experts",
    )(block_e, n_valid, xs, wu, bu, wd, bd)


def _slot_tile_copy(dest_hbm, dsm, sem, i, td):
    start = pl.multiple_of(i * (td * TOP_K), td * TOP_K)
    return pltpu.make_async_copy(dest_hbm.at[pl.ds(start, td * TOP_K)], dsm, sem)


def _token_rows(ref, idx, lead=()):
    start = pl.multiple_of(idx * TOKEN_TILE_ROWS, TOKEN_TILE_ROWS)
    return ref.at[lead + (pl.ds(start, TOKEN_TILE_ROWS), slice(None))]


def _combine_kernel(dest_hbm, ob_hbm, x1_ref, gate_ref, gf_ref, y1_ref, y2_ref, dsm, rows, sem_d, sem_r, *,
                    td, first_tiles):
    i = pl.program_id(0)
    slots = _slot_tile_copy(dest_hbm, dsm, sem_d, i, td)
    slots.start()
    slots.wait()

    def issue(r, carry):
        for k in range(TOP_K):
            pltpu.make_async_copy(_token_rows(ob_hbm, dsm[r * TOP_K + k]), _token_rows(rows, r, (k,)),
                                  sem_r).start(priority=k % 2)
        return carry

    lax.fori_loop(0, td, issue, 0, unroll=8)
    for k in range(TOP_K):
        pltpu.make_async_copy(ob_hbm.at[pl.ds(0, td * TOKEN_TILE_ROWS), :], rows.at[k], sem_r).wait()

    gate = gate_ref[...]
    parts = [_load_token_tiles(rows, td, (k,)) for k in range(TOP_K)]
    pieces = []
    for s in range(TOKEN_TILE_ROWS):
        moe = gate[:, 0:1] * parts[0][s]
        for k in range(1, TOP_K):
            moe = moe + gate[:, k:k + 1] * parts[k][s]
        pieces.append(x1_ref[:, s * LANES:(s + 1) * LANES] + moe)
    y = _rms(jnp.concatenate(pieces, axis=1), gf_ref[...])

    @pl.when(i < first_tiles)
    def _first():
        y1_ref[...] = y

    @pl.when(i >= first_tiles)
    def _second():
        y2_ref[...] = y


def _combine(x1, out_buf, dest_flat, gate, g_final, n_first):
    n, d = x1.shape
    td = COMBINE_TILE
    assert n % td == 0 and n_first % td == 0 and 0 < n_first < n
    t1 = n_first // td
    return pl.pallas_call(
        functools.partial(_combine_kernel, td=td, first_tiles=t1),
        grid=(n // td,),
        in_specs=[pl.BlockSpec(memory_space=pl.ANY), pl.BlockSpec(memory_space=pl.ANY),
                  pl.BlockSpec((td, d), lambda i: (i, 0)),
                  pl.BlockSpec((td, EXPERT_LANES), lambda i: (i, 0)),
                  _const_spec((1, d))],
        out_specs=[pl.BlockSpec((td, d), lambda i: (jnp.minimum(i, t1 - 1), 0)),
                   pl.BlockSpec((td, d), lambda i: (jnp.maximum(i - t1, 0), 0))],
        out_shape=[jax.ShapeDtypeStruct((n_first, d), F32), jax.ShapeDtypeStruct((n - n_first, d), F32)],
        scratch_shapes=[pltpu.SMEM((td * TOP_K,), jnp.int32),
                        pltpu.VMEM((TOP_K, td * TOKEN_TILE_ROWS, LANES), F32),
                        pltpu.SemaphoreType.DMA, pltpu.SemaphoreType.DMA],
        compiler_params=_cparams(("arbitrary",), 32),
        name="combine_norm",
    )(dest_flat, out_buf, x1, gate, g_final.reshape(1, d))


def _dispatch_kernel(zrow_ref, dest_hbm, h2_ref, xs_hbm, dsm, zbuf, sem_d, sem_z, sem_r, *, td):
    i = pl.program_id(0)
    block_rows = MOE_ROWS * TOKEN_TILE_ROWS

    def zero_copy(e):
        return pltpu.make_async_copy(zbuf, xs_hbm.at[pl.ds(_tile_start(zrow_ref[e]), block_rows), :], sem_z)

    def tail_copy(blk):
        return pltpu.make_async_copy(zbuf, xs_hbm.at[pl.ds(_tile_start(blk * MOE_ROWS), block_rows), :], sem_z)

    @pl.when(i == 0)
    def _clear_unwritten_blocks():
        zbuf[...] = jnp.zeros_like(zbuf)
        first_unused = zrow_ref[N_EXPERTS]
        n_blocks = xs_hbm.shape[0] // block_rows
        for e in range(N_EXPERTS):
            @pl.when(zrow_ref[e] >= 0)
            def _start(e=e):
                zero_copy(e).start()
        lax.fori_loop(first_unused, n_blocks, lambda blk, c: (tail_copy(blk).start(), c)[1], 0)
        for e in range(N_EXPERTS):
            @pl.when(zrow_ref[e] >= 0)
            def _wait(e=e):
                zero_copy(e).wait()
        lax.fori_loop(first_unused, n_blocks, lambda blk, c: (tail_copy(blk).wait(), c)[1], 0)

    slots = _slot_tile_copy(dest_hbm, dsm, sem_d, i, td)
    slots.start()
    slots.wait()

    def issue(r, carry):
        src = _token_rows(h2_ref, r)
        for k in range(TOP_K):
            pltpu.make_async_copy(src, _token_rows(xs_hbm, dsm[r * TOP_K + k]), sem_r).start(priority=k % 2)
        return carry

    lax.fori_loop(0, td, issue, 0, unroll=8)
    for _ in range(TOP_K):
        pltpu.make_async_copy(h2_ref, xs_hbm.at[pl.ds(0, td * TOKEN_TILE_ROWS), :], sem_r).wait()


def _tile_start(slot):
    return pl.multiple_of(slot * TOKEN_TILE_ROWS, TOKEN_TILE_ROWS)


def _dispatch(h2_tiles, dest_flat, zero_rows, n_slots):
    td = DISPATCH_TILE
    n = h2_tiles.shape[0] // TOKEN_TILE_ROWS
    assert n % td == 0
    grid_spec = pltpu.PrefetchScalarGridSpec(
        num_scalar_prefetch=1,
        grid=(n // td,),
        in_specs=[pl.BlockSpec(memory_space=pl.ANY),
                  pl.BlockSpec((td * TOKEN_TILE_ROWS, LANES), lambda i, zr: (i, 0))],
        out_specs=pl.BlockSpec(memory_space=pl.ANY),
        scratch_shapes=[pltpu.SMEM((td * TOP_K,), jnp.int32),
                        pltpu.VMEM((MOE_ROWS * TOKEN_TILE_ROWS, LANES), F32),
                        pltpu.SemaphoreType.DMA, pltpu.SemaphoreType.DMA, pltpu.SemaphoreType.DMA])
    return pl.pallas_call(
        functools.partial(_dispatch_kernel, td=td),
        grid_spec=grid_spec,
        out_shape=jax.ShapeDtypeStruct((n_slots * TOKEN_TILE_ROWS, LANES), F32),
        compiler_params=_cparams(("arbitrary",), 32),
        name="dispatch",
    )(zero_rows, dest_flat, h2_tiles)


def _route_kernel(lg_ref, eid_ref, gate_ref, rank_ref, cnt_ref, run_scr):
    i = pl.program_id(0)

    @pl.when(i == 0)
    def _init():
        run_scr[...] = jnp.zeros_like(run_scr)

    x = lg_ref[...]
    tm = x.shape[0]
    lane = lax.broadcasted_iota(jnp.int32, x.shape, 1).astype(F32)
    vals, hots = [], []
    eid = jnp.zeros(x.shape, F32)
    for k in range(TOP_K):
        m = jnp.max(x, axis=-1, keepdims=True)
        idx = jnp.min(jnp.where(x == m, lane, float(EXPERT_LANES)), axis=-1, keepdims=True)
        hot = lane == idx
        vals.append(m)
        hots.append(hot)
        eid = jnp.where(lane == float(k), idx, eid)
        x = jnp.where(hot, -jnp.inf, x)
    es = [jnp.exp(v - vals[0]) for v in vals]
    den = es[0]
    for e in es[1:]:
        den = den + e
    gate = jnp.zeros(x.shape, F32)
    for k in range(TOP_K):
        gate = jnp.where(lane == float(k), es[k] / den, gate)

    cnt_tok = hots[0].astype(F32)
    for hot in hots[1:]:
        cnt_tok = cnt_tok + hot.astype(F32)
    earlier = lax.broadcasted_iota(jnp.int32, (tm, tm), 1) < lax.broadcasted_iota(jnp.int32, (tm, tm), 0)
    before = run_scr[...] + jnp.dot(earlier.astype(BF16), cnt_tok.astype(BF16), preferred_element_type=F32)
    rank = jnp.zeros(x.shape, F32)
    for k in range(TOP_K):
        rank = jnp.where(lane == float(k), jnp.sum(jnp.where(hots[k], before, 0.0), axis=-1, keepdims=True), rank)
    run_scr[...] = run_scr[...] + jnp.sum(cnt_tok, axis=0, keepdims=True)

    eid_ref[...] = eid.astype(jnp.int32)
    gate_ref[...] = gate
    rank_ref[...] = rank.astype(jnp.int32)
    cnt_ref[...] = run_scr[...].astype(jnp.int32)


def _route(logits):
    n = logits.shape[0]
    tm = ROW_TILE
    row = pl.BlockSpec((tm, EXPERT_LANES), lambda i: (i, 0))
    eid, gate, rank, cnt = pl.pallas_call(
        _route_kernel,
        grid=(n // tm,),
        in_specs=[row],
        out_specs=[row, row, row, _const_spec((1, EXPERT_LANES))],
        out_shape=[jax.ShapeDtypeStruct((n, EXPERT_LANES), jnp.int32), jax.ShapeDtypeStruct((n, EXPERT_LANES), F32),
                   jax.ShapeDtypeStruct((n, EXPERT_LANES), jnp.int32),
                   jax.ShapeDtypeStruct((1, EXPERT_LANES), jnp.int32)],
        scratch_shapes=[pltpu.VMEM((1, EXPERT_LANES), F32)],
        compiler_params=_cparams(("arbitrary",), 32),
        name="route",
    )(logits)

    counts = cnt[0, :N_EXPERTS]
    padded = (counts + MOE_ROWS - 1) // MOE_ROWS * MOE_ROWS
    p_end = jnp.cumsum(padded)
    p_start = p_end - padded
    experts = jnp.arange(N_EXPERTS, dtype=jnp.int32)
    start_of = jnp.sum(jnp.where(eid[:, :TOP_K, None] == experts, p_start, 0), axis=-1)
    dest_flat = (start_of + rank[:, :TOP_K]).reshape(n * TOP_K).astype(jnp.int32)
    n_blocks = -(-(n * TOP_K + N_EXPERTS * (MOE_ROWS - 1)) // MOE_ROWS)
    block_start = jnp.arange(n_blocks, dtype=jnp.int32) * MOE_ROWS
    block_e = jnp.minimum(jnp.sum(p_end[None, :] <= block_start[:, None], axis=1), N_EXPERTS - 1).astype(jnp.int32)
    n_valid = (p_end[-1] // MOE_ROWS).astype(jnp.int32).reshape(1)
    block_e = jnp.where(jnp.arange(n_blocks) < n_valid[0], block_e, block_e[n_valid[0] - 1])
    zero_rows = jnp.concatenate([jnp.where(padded > 0, p_end - MOE_ROWS, -1), n_valid]).astype(jnp.int32)
    return gate, dest_flat, block_e, n_valid, zero_rows, n_blocks * MOE_ROWS


def _in_splits(dt, u_copy, v_layout, with_keys):
    off = np.cumsum([0, D_MODEL, Q_WIDTH, K_WIDTH, V_WIDTH, D_MODEL, N_BRANCH * D_MODEL])
    q_scale = HEAD_DIM ** -0.5 * math.log2(math.e)
    keys = ((int(off[2]), K_WIDTH, 1.0, ((F32, PLAIN),)),) if with_keys else ()
    return ((int(off[0]), D_MODEL, 1.0, (u_copy,)),
            (int(off[1]), Q_WIDTH, q_scale, ((dt, PLAIN),)),
            *keys,
            (int(off[3]), V_WIDTH, 1.0, ((F32, v_layout), (BF16, WITH_ONES))),
            (int(off[4]), D_MODEL, X_DIM ** -0.5, ((dt, PLAIN),)),
            (int(off[5]), N_BRANCH * D_MODEL, 1.0, ((F32, PLAIN),)))


def kernel(x_prompt, x_sample, mem_prompt, cache_k, cache_v, cache_mem_k, cache_mem_v, state_pool, page_table, g_mix, w_in, w_pool_group, pool_scale, lambda_q1, lambda_k1, lambda_q2, lambda_k2, g_subln, g_mem, w_mem_kv, w_pool_proj, w_diff_proj, w_cross_proj, w_o, g_ffn, w_router, b_router, w_up, b_up, w_down, b_down, g_final):
    b, s, d = x_prompt.shape
    db, t, _ = x_sample.shape
    depth = w_in.shape[0]
    assert depth == 1, "the step is written for the single-layer trunk this problem states"
    n_phys, page = cache_k.shape[1], cache_k.shape[2]
    n_mem = mem_prompt.shape[1]
    n_p, n_s = b * s, db * t

    cache_kt = cache_k.transpose(0, 1, 3, 4, 5, 2).reshape(depth * n_phys * K_WIDTH, page)
    cache_v2 = cache_v.reshape(depth * n_phys * page * N_KV_HEADS, V_DIM)
    cmk = _memory_rows(cache_mem_k)
    cmv = _memory_rows(cache_mem_v)

    xp = x_prompt.reshape(n_p, d)
    xs = x_sample.reshape(n_s, d)
    outs = {name: [] for name in ("kp", "vp", "mk", "mv", "hp", "ks", "vs", "hs")}

    for l in range(depth):
        lam_init = 0.8 - 0.6 * math.exp(-0.3 * l)
        lam_vecs = jnp.stack([lambda_q1[l], lambda_k1[l], lambda_q2[l], lambda_k2[l]]).astype(F32)
        w_in_bf = w_in[l].astype(BF16)
        wr = jnp.zeros((d, EXPERT_LANES), BF16).at[:, :N_EXPERTS].set(w_router[l].astype(BF16))
        br = jnp.full((1, EXPERT_LANES), MASK_VALUE, F32).at[0, :N_EXPERTS].set(b_router[l].astype(F32))
        wts = dict(wg=w_pool_group[l].astype(BF16), pool_scale=pool_scale[l].reshape(1, d).astype(F32),
                   wpp=w_pool_proj[l].astype(BF16), wdp=w_diff_proj[l].astype(BF16),
                   wcp=w_cross_proj[l].astype(BF16), wo=w_o[l].astype(BF16),
                   g_ffn=g_ffn[l].reshape(1, d).astype(F32), wr=wr, br=br)

        mk, mv = _rms_proj(mem_prompt.reshape(b * n_mem, d), g_mem[l], w_mem_kv[l].astype(BF16),
                           ((0, d, 1.0, ((F32, MEMORY_ROWS),)), (d, d, 1.0, ((F32, MEMORY_ROWS),))))

        k_off = D_MODEL + Q_WIDTH
        wk_t = w_in[l][:, k_off:k_off + K_WIDTH].T.astype(BF16)
        pooled_p, u_tail_p, q_p, v_p, v_pb, qx_p, gl_p, kt_p, kt_pb = _rms_proj(
            xp, g_mix[l], w_in_bf, _in_splits(BF16, (BF16, POOLED), VALUE_ROWS, with_keys=False), wt_bf=wk_t,
            transposed=(F32, BF16), tokens_per_seq=s)
        adiff_p = _attn_prompt(q_p.reshape(b, s, Q_WIDTH), kt_pb, v_pb.reshape(b, s, 2 * V_WIDTH), lam_vecs,
                               g_subln[l], lam_init).reshape(n_p, d)
        across_p = _cross(qx_p.reshape(b, s, d), mk, mv, n_mem, 0, BF16).reshape(n_p, d)

        u_s, q_s, k_s, v_s, _, qx_s, gl_s = _rms_proj(xs, g_mix[l], w_in_bf,
                                                           _in_splits(F32, (F32, PLAIN), PLAIN, with_keys=True))
        pooled_s = _pool_sample(state_pool[l], u_s.reshape(db, t, d)).reshape(n_s, d)
        adiff_s = _attn_sample(q_s.reshape(db, t, Q_WIDTH), k_s.reshape(db, t, K_WIDTH), v_s.reshape(db, t, V_WIDTH),
                               cache_kt, cache_v2, page, page_table + l * n_phys, lam_vecs, g_subln[l],
                               lam_init).reshape(n_s, d)
        across_s = _cross(qx_s.reshape(db, t, d), cmk, cmv, n_mem, l * db, F32).reshape(n_s, d)

        x1, h2, logits = _merge((pooled_p, adiff_p, across_p, gl_p, xp), (pooled_s, adiff_s, across_s, gl_s, xs), wts)
        gate, dest_flat, block_e, n_valid, zero_rows, n_slots = _route(logits)
        out_buf = _experts(_dispatch(h2, dest_flat, zero_rows, n_slots), block_e, n_valid, w_up[l],
                           b_up[l].reshape(N_EXPERTS, 1, 2 * D_FF), w_down[l], b_down[l].reshape(N_EXPERTS, 1, d))
        xp, xs = _combine(x1, out_buf, dest_flat, gate, g_final, n_p)

        outs["kp"].append(kt_p.reshape(b, N_KV_HEADS, 2, HEAD_DIM, s).transpose(0, 4, 1, 2, 3))
        outs["vp"].append(v_p.reshape(b, s, N_KV_HEADS, V_DIM))
        outs["mk"].append(_memory_from_rows(mk, 1, b, n_mem)[0])
        outs["mv"].append(_memory_from_rows(mv, 1, b, n_mem)[0])
        outs["hp"].append(u_tail_p[:, -POOL_STATE:].astype(state_pool.dtype))
        outs["ks"].append(k_s.reshape(db, t, N_KV_HEADS, 2, HEAD_DIM))
        outs["vs"].append(v_s.reshape(db, t, N_KV_HEADS, V_DIM))
        outs["hs"].append(jnp.concatenate([state_pool[l], u_s.reshape(db, t, d).astype(state_pool.dtype)],
                                          axis=1)[:, -POOL_STATE:])

    return (xp.reshape(b, s, d), xs.reshape(db, t, d), jnp.stack(outs["kp"]), jnp.stack(outs["vp"]),
            jnp.stack(outs["mk"]), jnp.stack(outs["mv"]), jnp.stack(outs["hp"]), jnp.stack(outs["ks"]),
            jnp.stack(outs["vs"]), jnp.stack(outs["hs"]))
```

```python
import functools
import math

import numpy as np
import jax
import jax.numpy as jnp
from jax import lax
from jax.experimental import pallas as pl
from jax.experimental.pallas import tpu as pltpu

F32 = jnp.float32
BF16 = jnp.bfloat16

D_MODEL = 1024
N_HEADS = 8
HEAD_DIM = 64
N_KV_HEADS = 4
GQA_GROUP = N_HEADS // N_KV_HEADS
V_DIM = 2 * HEAD_DIM
POOL_WINDOWS = (2, 4, 8, 16)
POOL_GROUPS = 4
POOL_GROUP_DIM = D_MODEL // POOL_GROUPS
POOL_STATE = 15
POOL_HALO = 16
X_HEADS = 4
X_DIM = D_MODEL // X_HEADS
N_BRANCH = 3
N_EXPERTS = 32
TOP_K = 4
D_FF = D_MODEL
SWIGLU_LIMIT = 7.0
SWIGLU_ALPHA = 1.702
RMS_EPS = 1e-6
Q_WIDTH = N_HEADS * 2 * HEAD_DIM
K_WIDTH = N_KV_HEADS * 2 * HEAD_DIM
V_WIDTH = N_KV_HEADS * V_DIM
KV_BLOCK = 2 * HEAD_DIM
MASK_VALUE = -1e30

LANES = 128
EXPERT_LANES = LANES
MIB = 2 ** 20

ROW_TILE = 256
ATTN_TQ = 512
ATTN_TK = 512
ATTN_ITEM_ROWS = 512
PAGES_PER_STEP = 32
CROSS_TQ = 1024
CROSS_SEQS = 4
MOE_ROWS = 512
DISPATCH_TILE = 1024
COMBINE_TILE = 512
POOL_SEQ_BLOCK = 8


def _cparams(semantics, vmem_mib):
    return pltpu.CompilerParams(dimension_semantics=semantics, vmem_limit_bytes=vmem_mib * MIB)


def _const_spec(shape):
    zeros = (0,) * len(shape)
    return pl.BlockSpec(shape, lambda *_: zeros)


def _rms(xf, g):
    return xf * lax.rsqrt(jnp.mean(xf * xf, axis=-1, keepdims=True) + RMS_EPS) * g


TOKEN_TILE_ROWS = D_MODEL // LANES


def _store_token_tiles(ref, x):
    n = x.shape[0]
    for s in range(TOKEN_TILE_ROWS):
        ref[pl.ds(s, n, stride=TOKEN_TILE_ROWS), :] = x[:, s * LANES:(s + 1) * LANES]


def _load_token_tiles(ref, n, lead=()):
    return [ref[lead + (pl.ds(s, n, stride=TOKEN_TILE_ROWS), slice(None))] for s in range(TOKEN_TILE_ROWS)]


PLAIN = ("plain",)
WITH_ONES = ("with_ones",)


def _chunk_rows(positions):
    return ("chunk_rows", tuple(positions))


VALUE_ROWS = _chunk_rows(range(N_KV_HEADS))
MEMORY_ROWS = _chunk_rows([(q % 2) * X_HEADS + q // 2 for q in range(X_HEADS * 2)])


POOLED = ("pooled",)


POOL_HISTORY = 128


def _pool_rows(ue, tok, o_ref, first_pos):
    tm = tok.shape[0]
    rows = ue.shape[0]
    own = lax.broadcasted_iota(jnp.int32, (tm, rows), 0) + POOL_HISTORY
    col = lax.broadcasted_iota(jnp.int32, (tm, rows), 1)
    e = ue[...]
    head = e.astype(BF16)
    rest = (e - head.astype(F32)).astype(BF16)
    pos = first_pos + lax.broadcasted_iota(jnp.int32, (tm, 1), 0)
    for g, w in enumerate(POOL_WINDOWS):
        c = slice(g * POOL_GROUP_DIM, (g + 1) * POOL_GROUP_DIM)
        band = jnp.where(jnp.logical_and(col <= own, col > own - w), 1.0, 0.0).astype(BF16)
        total = (jnp.dot(band, head[:, c], preferred_element_type=F32)
                 + jnp.dot(band, rest[:, c], preferred_element_type=F32))
        cnt = jnp.minimum(pos + 1, w).astype(F32)
        o_ref[:, c] = (total / cnt - tok[:, c]).astype(o_ref.dtype)


def _rms_proj_kernel(x_ref, g_ref, w_ref, *rest, splits, transposed, tiles_per_seq):
    hb = _rms(x_ref[...].astype(F32), g_ref[...]).astype(BF16)
    if transposed:
        wt_ref, rest = rest[0], rest[1:]
    out_refs = rest
    tm = x_ref.shape[0]
    oi = 0
    for c0, width, scale, copies in splits:
        z = jnp.dot(hb, w_ref[:, c0:c0 + width], preferred_element_type=F32)
        if scale != 1.0:
            z = z * scale
        for dt, layout in copies:
            o_ref = out_refs[oi]
            oi += 1
            if layout == POOLED:
                tail_ref, ue = out_refs[oi], out_refs[-1]
                oi += 1
                tile_in_seq = pl.program_id(0) % tiles_per_seq
                @pl.when(tile_in_seq == 0)
                def _sequence_start():
                    ue[0:POOL_HISTORY, :] = jnp.zeros((POOL_HISTORY, ue.shape[1]), F32)

                @pl.when(tile_in_seq != 0)
                def _carry_history():
                    ue[0:POOL_HISTORY, :] = ue[tm:tm + POOL_HISTORY, :]

                ue[POOL_HISTORY:, :] = z
                _pool_rows(ue, z, o_ref, tile_in_seq * tm)
                tail_ref[...] = z[tm - POOL_HALO:, :]
            elif layout == WITH_ONES:
                for h in range(width // V_DIM):
                    o_ref[:, 2 * h * V_DIM:(2 * h + 1) * V_DIM] = z[:, h * V_DIM:(h + 1) * V_DIM].astype(dt)
                    o_ref[:, (2 * h + 1) * V_DIM:(2 * h + 2) * V_DIM] = jnp.ones((z.shape[0], V_DIM), dt)
            elif layout[0] == "chunk_rows":
                positions = layout[1]
                for q, pos in enumerate(positions):
                    o_ref[pl.ds(pos, z.shape[0], stride=len(positions)), :] = z[:, q * LANES:(q + 1) * LANES].astype(dt)
            else:
                o_ref[...] = z.astype(dt)
    if transposed:
        zt = lax.dot_general(wt_ref[...], hb, (((1,), (1,)), ((), ())), preferred_element_type=F32)
        for dt in transposed:
            out_refs[oi][...] = zt.astype(dt)
            oi += 1


def _rms_proj(x2d, g, w_bf, splits, wt_bf=None, transposed=(), tokens_per_seq=None):
    n, d = x2d.shape
    tm = ROW_TILE
    assert n % tm == 0
    tiles_per_seq = None
    if tokens_per_seq is not None:
        assert tokens_per_seq % tm == 0 and n % tokens_per_seq == 0
        tiles_per_seq = tokens_per_seq // tm
    out_shape, out_specs, scratch = [], [], []
    for _, width, _, copies in splits:
        for dt, layout in copies:
            if layout == POOLED:
                out_shape += [jax.ShapeDtypeStruct((n, width), dt),
                              jax.ShapeDtypeStruct((n // tokens_per_seq, POOL_HALO, width), F32)]
                out_specs += [pl.BlockSpec((tm, width), lambda i: (i, 0)),
                              pl.BlockSpec((None, POOL_HALO, width), lambda i: (i // tiles_per_seq, 0, 0))]
                assert tm >= POOL_HISTORY >= POOL_WINDOWS[-1] - 1
                scratch.append(pltpu.VMEM((POOL_HISTORY + tm, width), F32))
                continue
            if layout[0] == "chunk_rows":
                n_chunks = len(layout[1])
                assert width == n_chunks * LANES
                shape, block = (n * n_chunks, LANES), (tm * n_chunks, LANES)
            else:
                w_out = 2 * width if layout == WITH_ONES else width
                shape, block = (n, w_out), (tm, w_out)
            out_shape.append(jax.ShapeDtypeStruct(shape, dt))
            out_specs.append(pl.BlockSpec(block, lambda i: (i, 0)))
    operands = [x2d, g.reshape(1, d), w_bf]
    in_specs = [pl.BlockSpec((tm, d), lambda i: (i, 0)), _const_spec((1, d)), _const_spec(w_bf.shape)]
    if transposed:
        feats = wt_bf.shape[0]
        operands.append(wt_bf)
        in_specs.append(_const_spec(wt_bf.shape))
        for dt in transposed:
            out_shape.append(jax.ShapeDtypeStruct((n // tokens_per_seq, feats, tokens_per_seq), dt))
            out_specs.append(pl.BlockSpec((None, feats, tm), lambda i: (i // tiles_per_seq, 0, i % tiles_per_seq)))
    return pl.pallas_call(
        functools.partial(_rms_proj_kernel, splits=splits, transposed=tuple(transposed),
                          tiles_per_seq=tiles_per_seq),
        grid=(n // tm,),
        in_specs=in_specs,
        out_specs=out_specs,
        out_shape=out_shape,
        scratch_shapes=scratch,
        compiler_params=_cparams(("arbitrary",), 56),
        name="rms_proj",
    )(*operands)


def _pool_sample_kernel(ue_ref, o_ref, *, n_new):
    lo = ue_ref.shape[1] - n_new
    for s in range(ue_ref.shape[0]):
        for g, w in enumerate(POOL_WINDOWS):
            c = slice(g * POOL_GROUP_DIM, (g + 1) * POOL_GROUP_DIM)
            tok = ue_ref[s, lo:lo + n_new, c]
            acc = tok
            for j in range(1, w):
                acc = acc + ue_ref[s, lo - j:lo - j + n_new, c]
            o_ref[s, :, c] = acc / float(w) - tok


def _pool_sample(hist, u3):
    db, t, d = u3.shape
    assert hist.shape[1] >= POOL_WINDOWS[-1] - 1
    pad = (-(hist.shape[1] + t)) % 8
    ue = jnp.concatenate([jnp.zeros((db, pad, d), F32), hist.astype(F32), u3], axis=1)
    rows = ue.shape[1]
    sb = POOL_SEQ_BLOCK
    return pl.pallas_call(
        functools.partial(_pool_sample_kernel, n_new=t),
        grid=(db // sb,),
        in_specs=[pl.BlockSpec((sb, rows, d), lambda i: (i, 0, 0))],
        out_specs=pl.BlockSpec((sb, t, d), lambda i: (i, 0, 0)),
        out_shape=jax.ShapeDtypeStruct((db, t, d), F32),
        compiler_params=_cparams(("parallel",), 32),
        name="pool_sample",
    )(ue)


def _lambda_value(lam_ref, lam_init):
    a = jnp.sum(lam_ref[0:1, :] * lam_ref[1:2, :], axis=-1, keepdims=True)
    b = jnp.sum(lam_ref[2:3, :] * lam_ref[3:4, :], axis=-1, keepdims=True)
    return jnp.exp(a) - jnp.exp(b) + lam_init


def _subln(o, gsub, lam_init):
    return _rms(o, gsub) * (1.0 - lam_init)


def _softmax_update(s, m_prev):
    m_new = jnp.maximum(m_prev, jnp.max(s, axis=-1, keepdims=True))
    alpha = jnp.exp2(m_prev - m_new)
    p = jnp.exp2(s - jnp.tile(m_new, (1, s.shape[1] // LANES)))
    return p, m_new, alpha


def _attn_prompt_kernel(qi_ref, ki_ref, last_ref, q_ref, k_ref, v_ref, lam_ref, gsub_ref, o_ref,
                        qpad, acc, m_scr, *, tq, tk, lam_init):
    step = pl.program_id(1)
    qi = qi_ref[step]
    ki = ki_ref[step]
    n_blocks = 2 * GQA_GROUP

    @pl.when(ki == 0)
    def _init():
        lane = lax.broadcasted_iota(jnp.int32, (tq, KV_BLOCK), 1)
        for h in range(N_KV_HEADS):
            for g in range(GQA_GROUP):
                c0 = (h * GQA_GROUP + g) * KV_BLOCK
                blk = q_ref[:, c0:c0 + KV_BLOCK]
                qpad[h, g * tq:(g + 1) * tq, :] = jnp.where(lane < HEAD_DIM, blk, jnp.zeros_like(blk))
                qpad[h, (GQA_GROUP + g) * tq:(GQA_GROUP + g + 1) * tq, :] = jnp.where(lane >= HEAD_DIM, blk,
                                                                                     jnp.zeros_like(blk))
        acc[...] = jnp.zeros_like(acc)
        m_scr[...] = jnp.full_like(m_scr, MASK_VALUE)

    items = [(h, r0) for h in range(N_KV_HEADS) for r0 in range(0, n_blocks * tq, ATTN_ITEM_ROWS)]

    def item_rows(r0):
        return slice(r0, r0 + ATTN_ITEM_ROWS)

    def scores(item):
        h, r0 = item
        return jnp.dot(qpad[h, item_rows(r0), :], k_ref[h * KV_BLOCK:(h + 1) * KV_BLOCK, :],
                       preferred_element_type=F32)

    def body(masked):
        if masked:
            kpos = ki * tk + lax.broadcasted_iota(jnp.int32, (ATTN_ITEM_ROWS, tk), 1)
            q_in_item = lax.broadcasted_iota(jnp.int32, (ATTN_ITEM_ROWS, tk), 0)
        s_next = scores(items[0])
        for n, (h, r0) in enumerate(items):
            s = s_next
            if n + 1 < len(items):
                s_next = scores(items[n + 1])
            if masked:
                s = jnp.where(kpos <= qi * tq + r0 % tq + q_in_item, s, MASK_VALUE)
            rows = item_rows(r0)
            p, m_new, alpha = _softmax_update(s, m_scr[h, rows, :])
            m_scr[h, rows, :] = m_new
            pv = jnp.dot(p.astype(BF16), v_ref[:, h * 2 * V_DIM:(h + 1) * 2 * V_DIM], preferred_element_type=F32)
            acc[h, rows, :] = jnp.tile(alpha, (1, 2)) * acc[h, rows, :] + pv

    needs_mask = (ki + 1) * tk - 1 > qi * tq

    @pl.when(needs_mask)
    def _masked():
        body(True)

    @pl.when(jnp.logical_not(needs_mask))
    def _unmasked():
        body(False)

    @pl.when(last_ref[step] == 1)
    def _finish():
        lam = _lambda_value(lam_ref, lam_init)
        gsub = gsub_ref[...]
        for h in range(N_KV_HEADS):
            for g in range(GQA_GROUP):
                rows1 = slice(g * tq, (g + 1) * tq)
                rows2 = slice((GQA_GROUP + g) * tq, (GQA_GROUP + g + 1) * tq)
                o1 = acc[h, rows1, 0:V_DIM] / acc[h, rows1, V_DIM:2 * V_DIM]
                o2 = acc[h, rows2, 0:V_DIM] / acc[h, rows2, V_DIM:2 * V_DIM]
                c0 = (h * GQA_GROUP + g) * V_DIM
                o_ref[:, c0:c0 + V_DIM] = _subln(o1 - lam * o2, gsub, lam_init).astype(o_ref.dtype)


def _attn_prompt(q, k, v, lam_vecs, gsub, lam_init):
    b, s, _ = q.shape
    tq, tk = ATTN_TQ, ATTN_TK
    assert s % tq == 0 and s % tk == 0
    qi_l, ki_l, last_l = [], [], []
    for qi in range(s // tq):
        n_k = min(((qi + 1) * tq - 1) // tk + 1, s // tk)
        for ki in range(n_k):
            qi_l.append(qi)
            ki_l.append(ki)
            last_l.append(1 if ki == n_k - 1 else 0)
    tabs = [jnp.asarray(np.array(t, np.int32)) for t in (qi_l, ki_l, last_l)]
    rows = 2 * GQA_GROUP * tq
    assert tq % ATTN_ITEM_ROWS == 0
    grid_spec = pltpu.PrefetchScalarGridSpec(
        num_scalar_prefetch=3,
        grid=(b, len(qi_l)),
        in_specs=[pl.BlockSpec((None, tq, Q_WIDTH), lambda bi, st, qt, kt, lt: (bi, qt[st], 0)),
                  pl.BlockSpec((None, K_WIDTH, tk), lambda bi, st, qt, kt, lt: (bi, 0, kt[st])),
                  pl.BlockSpec((None, tk, 2 * V_WIDTH), lambda bi, st, qt, kt, lt: (bi, kt[st], 0)),
                  _const_spec((4, HEAD_DIM)), _const_spec((1, V_DIM))],
        out_specs=pl.BlockSpec((None, tq, N_HEADS * V_DIM), lambda bi, st, qt, kt, lt: (bi, qt[st], 0)),
        scratch_shapes=[pltpu.VMEM((N_KV_HEADS, rows, KV_BLOCK), BF16),
                        pltpu.VMEM((N_KV_HEADS, rows, 2 * V_DIM), F32),
                        pltpu.VMEM((N_KV_HEADS, rows, LANES), F32)])
    return pl.pallas_call(
        functools.partial(_attn_prompt_kernel, tq=tq, tk=tk, lam_init=lam_init),
        grid_spec=grid_spec,
        out_shape=jax.ShapeDtypeStruct((b, s, N_HEADS * V_DIM), BF16),
        compiler_params=_cparams(("parallel", "arbitrary"), 56),
        name="attn_prompt",
    )(*tabs, q, k, v, lam_vecs, gsub.reshape(1, V_DIM))


SAMPLE_ROWS = N_KV_HEADS * 2 * GQA_GROUP


def _attn_sample_kernel(pt_ref, wq_ref, kn_ref, vn_ref, lam_ref, gsub_ref, kt_hbm, v_hbm, o_ref,
                        kbuf, vbuf, acc, m_scr, l_scr, sem_k, sem_v, *, n_pages, t_new, page, lam_init):
    seq = pl.program_id(0)
    step = pl.program_id(1)
    n_steps = pl.num_programs(1)
    rows = SAMPLE_ROWS * t_new
    k_rows = kbuf.shape[1] // n_pages
    v_rows = vbuf.shape[1] // n_pages
    flat = seq * n_steps + step
    slot = flat % 2

    def page_copies(seq_i, step_i, slot_i):
        copies = []
        for j in range(n_pages):
            pid = pt_ref[seq_i, step_i * n_pages + j]
            copies.append(pltpu.make_async_copy(kt_hbm.at[pl.ds(pl.multiple_of(pid * k_rows, k_rows), k_rows), :],
                                                kbuf.at[slot_i, pl.ds(j * k_rows, k_rows), :], sem_k.at[slot_i]))
            copies.append(pltpu.make_async_copy(v_hbm.at[pl.ds(pl.multiple_of(pid * v_rows, v_rows), v_rows), :],
                                                vbuf.at[slot_i, pl.ds(j * v_rows, v_rows), :], sem_v.at[slot_i]))
        return copies

    @pl.when(flat == 0)
    def _first_fetch():
        for c in page_copies(seq, step, slot):
            c.start()

    @pl.when(flat + 1 < pl.num_programs(0) * n_steps)
    def _prefetch():
        wrap = step + 1 == n_steps
        for c in page_copies(jnp.where(wrap, seq + 1, seq), jnp.where(wrap, 0, step + 1), 1 - slot):
            c.start()

    @pl.when(step == 0)
    def _init():
        acc[...] = jnp.zeros_like(acc)
        m_scr[...] = jnp.full_like(m_scr, MASK_VALUE)
        l_scr[...] = jnp.zeros_like(l_scr)

    pltpu.make_async_copy(kt_hbm.at[pl.ds(0, n_pages * k_rows), :], kbuf.at[slot], sem_k.at[slot]).wait()
    pltpu.make_async_copy(v_hbm.at[pl.ds(0, n_pages * v_rows), :], vbuf.at[slot], sem_v.at[slot]).wait()

    wq = wq_ref[...]

    def update(s, v_list):
        p, m_new, alpha = _softmax_update(s, m_scr[...])
        m_scr[...] = m_new
        l_scr[...] = alpha * l_scr[...] + jnp.sum(p, axis=-1, keepdims=True)
        pb = p.astype(BF16)
        pv = None
        for j, vj in enumerate(v_list):
            c = vj.shape[0]
            term = jnp.dot(pb[:, j * c:(j + 1) * c], vj, preferred_element_type=F32)
            pv = term if pv is None else pv + term
        acc[...] = jnp.tile(alpha, (1, N_KV_HEADS)) * acc[...] + pv

    def page_keys(j):
        return kbuf[slot, pl.ds(j * k_rows, k_rows), :].astype(BF16)

    def page_values(j):
        heads = [vbuf[slot, pl.ds(j * v_rows + h, page, stride=N_KV_HEADS), :] for h in range(N_KV_HEADS)]
        return jnp.concatenate(heads, axis=1).astype(BF16)

    s_list = [jnp.dot(wq, page_keys(j), preferred_element_type=F32) for j in range(n_pages)]
    update(jnp.concatenate(s_list, axis=1), [page_values(j) for j in range(n_pages)])

    @pl.when(step == pl.num_programs(1) - 1)
    def _finish():
        pad = jnp.zeros((LANES - t_new, kn_ref.shape[1]), F32)
        kn = jnp.concatenate([kn_ref[...], pad], axis=0).astype(BF16)
        vn = jnp.concatenate([vn_ref[...], pad], axis=0).astype(BF16)
        s_new = lax.dot_general(wq, kn, (((1,), (1,)), ((), ())), preferred_element_type=F32)
        t_of_row = lax.broadcasted_iota(jnp.int32, (rows, LANES), 0) % t_new
        key = lax.broadcasted_iota(jnp.int32, (rows, LANES), 1)
        update(jnp.where(key <= t_of_row, s_new, MASK_VALUE), [vn])

        lam = _lambda_value(lam_ref, lam_init)
        gsub = gsub_ref[...]
        for h in range(N_KV_HEADS):
            for g in range(GQA_GROUP):
                r1 = (h * 2 * GQA_GROUP + g) * t_new
                r2 = (h * 2 * GQA_GROUP + GQA_GROUP + g) * t_new
                cols = slice(h * V_DIM, (h + 1) * V_DIM)
                o1 = acc[r1:r1 + t_new, cols] / l_scr[r1:r1 + t_new, :]
                o2 = acc[r2:r2 + t_new, cols] / l_scr[r2:r2 + t_new, :]
                c0 = (h * GQA_GROUP + g) * V_DIM
                o_ref[:, c0:c0 + V_DIM] = _subln(o1 - lam * o2, gsub, lam_init).astype(o_ref.dtype)


def _sample_query_rows(q3):
    db, t, _ = q3.shape
    q6 = q3.reshape(db, t, N_KV_HEADS, GQA_GROUP, 2, HEAD_DIM).transpose(0, 2, 4, 3, 1, 5)
    eye = jnp.eye(N_KV_HEADS * 2, dtype=q3.dtype).reshape(N_KV_HEADS, 2, N_KV_HEADS, 2)
    wq = jnp.einsum("bhmgtd,hmHM->bhmgtHMd", q6, eye)
    return wq.reshape(db, SAMPLE_ROWS * t, K_WIDTH).astype(BF16)


def _attn_sample(q3, k_new, v_new, cache_kt, cache_v, page, page_ids, lam_vecs, gsub, lam_init):
    db, t, _ = q3.shape
    n_pages_total = page_ids.shape[1]
    g = PAGES_PER_STEP
    rows = SAMPLE_ROWS * t
    assert n_pages_total % g == 0 and rows % 8 == 0 and t <= LANES and page % LANES == 0
    wq = _sample_query_rows(q3)

    assert page == LANES and cache_kt.shape[1] == page and cache_v.shape[1] == V_DIM
    grid_spec = pltpu.PrefetchScalarGridSpec(
        num_scalar_prefetch=1,
        grid=(db, n_pages_total // g),
        in_specs=[pl.BlockSpec((None, rows, K_WIDTH), lambda bi, st, pt: (bi, 0, 0)),
                  pl.BlockSpec((None, t, K_WIDTH), lambda bi, st, pt: (bi, 0, 0)),
                  pl.BlockSpec((None, t, V_WIDTH), lambda bi, st, pt: (bi, 0, 0)),
                  _const_spec((4, HEAD_DIM)), _const_spec((1, V_DIM)),
                  pl.BlockSpec(memory_space=pl.ANY), pl.BlockSpec(memory_space=pl.ANY)],
        out_specs=pl.BlockSpec((None, t, N_HEADS * V_DIM), lambda bi, st, pt: (bi, 0, 0)),
        scratch_shapes=[pltpu.VMEM((2, g * K_WIDTH, page), F32),
                        pltpu.VMEM((2, g * page * N_KV_HEADS, V_DIM), F32),
                        pltpu.VMEM((rows, V_WIDTH), F32),
                        pltpu.VMEM((rows, LANES), F32),
                        pltpu.VMEM((rows, LANES), F32),
                        pltpu.SemaphoreType.DMA((2,)), pltpu.SemaphoreType.DMA((2,))])
    return pl.pallas_call(
        functools.partial(_attn_sample_kernel, n_pages=g, t_new=t, page=page, lam_init=lam_init),
        grid_spec=grid_spec,
        out_shape=jax.ShapeDtypeStruct((db, t, N_HEADS * V_DIM), F32),
        compiler_params=_cparams(("arbitrary", "arbitrary"), 56),
        name="attn_sample",
    )(page_ids, wq, k_new, v_new, lam_vecs, gsub.reshape(1, V_DIM), cache_kt, cache_v)


def _cross_kernel(q_ref, mk_ref, mv_ref, o_ref, *, n_mem):
    halves = X_DIM // LANES
    rows_per_token = X_HEADS * halves
    mem_rows = n_mem * rows_per_token

    def head(ref, seq, h):
        parts = [ref[pl.ds(seq * mem_rows + c * X_HEADS + h, n_mem, stride=rows_per_token), :]
                 for c in range(halves)]
        return jnp.concatenate(parts, axis=1).astype(BF16)

    for seq in range(q_ref.shape[0]):
        for h in range(X_HEADS):
            c = slice(h * X_DIM, (h + 1) * X_DIM)
            q = q_ref[seq, :, c].astype(BF16)
            s = lax.dot_general(q, head(mk_ref, seq, h), (((1,), (1,)), ((), ())), preferred_element_type=F32)
            p = jnp.exp(s - jnp.max(s, axis=-1, keepdims=True))
            l = jnp.sum(p, axis=-1, keepdims=True)
            o = jnp.dot(p.astype(BF16), head(mv_ref, seq, h), preferred_element_type=F32)
            o_ref[seq, :, c] = (o / l).astype(o_ref.dtype)


def _memory_rows(mem5):
    l, b, n_mem, heads, xd = mem5.shape
    m6 = mem5.reshape(l, b, n_mem, heads, xd // LANES, LANES).transpose(0, 1, 2, 4, 3, 5)
    return m6.reshape(l * b * n_mem * heads * (xd // LANES), LANES)


def _memory_from_rows(rows, l, b, n_mem):
    halves = X_DIM // LANES
    return rows.reshape(l, b, n_mem, halves, X_HEADS, LANES).transpose(0, 1, 2, 4, 3, 5).reshape(
        l, b, n_mem, X_HEADS, X_DIM)


def _cross(q3, mk, mv, n_mem, mem_offset, out_dtype):
    b, t, d = q3.shape
    tq = min(CROSS_TQ, t)
    seqs = CROSS_SEQS if (t == tq and b % CROSS_SEQS == 0 and mem_offset % CROSS_SEQS == 0) else 1
    mem_rows = seqs * n_mem * (d // LANES)
    mem_block = mem_offset // seqs
    return pl.pallas_call(
        functools.partial(_cross_kernel, n_mem=n_mem),
        grid=(b // seqs, t // tq),
        in_specs=[pl.BlockSpec((seqs, tq, d), lambda bi, i: (bi, i, 0)),
                  pl.BlockSpec((mem_rows, LANES), lambda bi, i: (bi + mem_block, 0)),
                  pl.BlockSpec((mem_rows, LANES), lambda bi, i: (bi + mem_block, 0))],
        out_specs=pl.BlockSpec((seqs, tq, d), lambda bi, i: (bi, i, 0)),
        out_shape=jax.ShapeDtypeStruct((b, t, d), out_dtype),
        compiler_params=_cparams(("parallel", "parallel"), 32),
        name="cross_attn",
    )(q3, mk, mv)


N_MERGE_INPUTS = 5


def _merge_kernel(*refs, first_tiles):
    first, second = refs[:N_MERGE_INPUTS], refs[N_MERGE_INPUTS:2 * N_MERGE_INPUTS]
    (wg_ref, ps_ref, wpp_ref, wdp_ref, wcp_ref, wo_ref, gffn_ref, wr_ref, br_ref,
     x1_ref, h2_ref, lg_ref) = refs[2 * N_MERGE_INPUTS:]

    def body(pooled_ref, adiff_ref, across_ref, gl_ref, x_ref):
        pooled = pooled_ref[...].astype(BF16)
        ys = [jnp.dot(pooled[:, g * POOL_GROUP_DIM:(g + 1) * POOL_GROUP_DIM], wg_ref[g],
                      preferred_element_type=F32) for g in range(POOL_GROUPS)]
        a_pool = jnp.concatenate(ys, axis=1) * ps_ref[...]
        p_pool = jnp.dot(a_pool.astype(BF16), wpp_ref[...], preferred_element_type=F32)
        p_diff = jnp.dot(adiff_ref[...].astype(BF16), wdp_ref[...], preferred_element_type=F32)
        p_cross = jnp.dot(across_ref[...].astype(BF16), wcp_ref[...], preferred_element_type=F32)
        merged = (jax.nn.sigmoid(gl_ref[:, 0:D_MODEL]) * p_pool
                  + jax.nn.sigmoid(gl_ref[:, D_MODEL:2 * D_MODEL]) * p_diff
                  + jax.nn.sigmoid(gl_ref[:, 2 * D_MODEL:3 * D_MODEL]) * p_cross)
        x1 = x_ref[...] + jnp.dot(merged.astype(BF16), wo_ref[...], preferred_element_type=F32)
        x1_ref[...] = x1
        h2 = _rms(x1, gffn_ref[...])
        _store_token_tiles(h2_ref, h2)
        lg_ref[...] = jnp.dot(h2.astype(BF16), wr_ref[...], preferred_element_type=F32) + br_ref[...]

    i = pl.program_id(0)

    @pl.when(i < first_tiles)
    def _first():
        body(*first)

    @pl.when(i >= first_tiles)
    def _second():
        body(*second)


def _merge(first, second, wts):
    d = D_MODEL
    tm = ROW_TILE
    n1, n2 = first[-1].shape[0], second[-1].shape[0]
    assert n1 % tm == 0 and n2 % tm == 0
    t1 = n1 // tm
    n = n1 + n2
    widths = (d, d, d, N_BRANCH * d, d)
    first_specs = [pl.BlockSpec((tm, w), lambda i: (jnp.minimum(i, t1 - 1), 0)) for w in widths]
    second_specs = [pl.BlockSpec((tm, w), lambda i: (jnp.maximum(i - t1, 0), 0)) for w in widths]
    row = lambda w: pl.BlockSpec((tm, w), lambda i: (i, 0))
    consts = [wts["wg"], wts["pool_scale"], wts["wpp"], wts["wdp"], wts["wcp"], wts["wo"], wts["g_ffn"],
              wts["wr"], wts["br"]]
    return pl.pallas_call(
        functools.partial(_merge_kernel, first_tiles=t1),
        grid=(n // tm,),
        in_specs=first_specs + second_specs + [_const_spec(c.shape) for c in consts],
        out_specs=[row(d), pl.BlockSpec((tm * TOKEN_TILE_ROWS, LANES), lambda i: (i, 0)), row(EXPERT_LANES)],
        out_shape=[jax.ShapeDtypeStruct((n, d), F32), jax.ShapeDtypeStruct((n * TOKEN_TILE_ROWS, LANES), F32),
                   jax.ShapeDtypeStruct((n, EXPERT_LANES), F32)],
        compiler_params=_cparams(("arbitrary",), 56),
        name="merge",
    )(*first, *second, *consts)


def _expert_kernel(be_ref, nv_ref, xs_ref, wu_ref, bu_ref, wd_ref, bd_ref, o_ref, wu_bf, wd_bf):
    i = pl.program_id(0)
    new_expert = jnp.logical_or(i == 0, be_ref[i] != be_ref[jnp.maximum(i - 1, 0)])

    @pl.when(jnp.logical_and(i < nv_ref[0], new_expert))
    def _cast_weights():
        wu_bf[...] = wu_ref[...].astype(BF16)
        wd_bf[...] = wd_ref[...].astype(BF16)

    @pl.when(i < nv_ref[0])
    def _run():
        x = jnp.concatenate(_load_token_tiles(xs_ref, MOE_ROWS), axis=1).astype(BF16)
        up = jnp.dot(x, wu_bf[...], preferred_element_type=F32) + bu_ref[...]
        glu = jnp.minimum(up[:, :D_FF], SWIGLU_LIMIT)
        lin = jnp.clip(up[:, D_FF:], -SWIGLU_LIMIT, SWIGLU_LIMIT)
        act = glu * jax.nn.sigmoid(SWIGLU_ALPHA * glu) * (lin + 1.0)
        _store_token_tiles(o_ref, jnp.dot(act.astype(BF16), wd_bf[...], preferred_element_type=F32) + bd_ref[...])

    @pl.when(i >= nv_ref[0])
    def _skip():
        o_ref[...] = jnp.zeros_like(o_ref)


def _experts(xs, block_e, n_valid, wu, bu, wd, bd):
    d = D_MODEL
    tile_rows = MOE_ROWS * TOKEN_TILE_ROWS
    grid_spec = pltpu.PrefetchScalarGridSpec(
        num_scalar_prefetch=2,
        grid=(xs.shape[0] // tile_rows,),
        in_specs=[pl.BlockSpec((tile_rows, LANES), lambda i, be, nv: (jnp.minimum(i, nv[0] - 1), 0)),
                  pl.BlockSpec((None, d, 2 * D_FF), lambda i, be, nv: (be[i], 0, 0)),
                  pl.BlockSpec((None, 1, 2 * D_FF), lambda i, be, nv: (be[i], 0, 0)),
                  pl.BlockSpec((None, D_FF, d), lambda i, be, nv: (be[i], 0, 0)),
                  pl.BlockSpec((None, 1, d), lambda i, be, nv: (be[i], 0, 0))],
        out_specs=pl.BlockSpec((tile_rows, LANES), lambda i, be, nv: (i, 0)),
        scratch_shapes=[pltpu.VMEM((d, 2 * D_FF), BF16), pltpu.VMEM((D_FF, d), BF16)])
    return pl.pallas_call(
        _expert_kernel,
        grid_spec=grid_spec,
        out_shape=jax.ShapeDtypeStruct(xs.shape, F32),
        compiler_params=_cparams(("arbitrary",), 56),
        name="experts",
    )(block_e, n_valid, xs, wu, bu, wd, bd)


def _slot_tile_copy(dest_hbm, dsm, sem, i, td):
    start = pl.multiple_of(i * (td * TOP_K), td * TOP_K)
    return pltpu.make_async_copy(dest_hbm.at[pl.ds(start, td * TOP_K)], dsm, sem)


def _token_rows(ref, idx, lead=()):
    start = pl.multiple_of(idx * TOKEN_TILE_ROWS, TOKEN_TILE_ROWS)
    return ref.at[lead + (pl.ds(start, TOKEN_TILE_ROWS), slice(None))]


def _combine_kernel(dest_hbm, ob_hbm, x1_ref, gate_ref, gf_ref, y1_ref, y2_ref, dsm, rows, sem_d, sem_r, *,
                    td, first_tiles):
    i = pl.program_id(0)
    slots = _slot_tile_copy(dest_hbm, dsm, sem_d, i, td)
    slots.start()
    slots.wait()

    def issue(r, carry):
        for k in range(TOP_K):
            pltpu.make_async_copy(_token_rows(ob_hbm, dsm[r * TOP_K + k]), _token_rows(rows, r, (k,)),
                                  sem_r).start(priority=k % 2)
        return carry

    lax.fori_loop(0, td, issue, 0, unroll=8)
    for k in range(TOP_K):
        pltpu.make_async_copy(ob_hbm.at[pl.ds(0, td * TOKEN_TILE_ROWS), :], rows.at[k], sem_r).wait()

    gate = gate_ref[...]
    parts = [_load_token_tiles(rows, td, (k,)) for k in range(TOP_K)]
    pieces = []
    for s in range(TOKEN_TILE_ROWS):
        moe = gate[:, 0:1] * parts[0][s]
        for k in range(1, TOP_K):
            moe = moe + gate[:, k:k + 1] * parts[k][s]
        pieces.append(x1_ref[:, s * LANES:(s + 1) * LANES] + moe)
    y = _rms(jnp.concatenate(pieces, axis=1), gf_ref[...])

    @pl.when(i < first_tiles)
    def _first():
        y1_ref[...] = y

    @pl.when(i >= first_tiles)
    def _second():
        y2_ref[...] = y


def _combine(x1, out_buf, dest_flat, gate, g_final, n_first):
    n, d = x1.shape
    td = COMBINE_TILE
    assert n % td == 0 and n_first % td == 0 and 0 < n_first < n
    t1 = n_first // td
    return pl.pallas_call(
        functools.partial(_combine_kernel, td=td, first_tiles=t1),
        grid=(n // td,),
        in_specs=[pl.BlockSpec(memory_space=pl.ANY), pl.BlockSpec(memory_space=pl.ANY),
                  pl.BlockSpec((td, d), lambda i: (i, 0)),
                  pl.BlockSpec((td, EXPERT_LANES), lambda i: (i, 0)),
                  _const_spec((1, d))],
        out_specs=[pl.BlockSpec((td, d), lambda i: (jnp.minimum(i, t1 - 1), 0)),
                   pl.BlockSpec((td, d), lambda i: (jnp.maximum(i - t1, 0), 0))],
        out_shape=[jax.ShapeDtypeStruct((n_first, d), F32), jax.ShapeDtypeStruct((n - n_first, d), F32)],
        scratch_shapes=[pltpu.SMEM((td * TOP_K,), jnp.int32),
                        pltpu.VMEM((TOP_K, td * TOKEN_TILE_ROWS, LANES), F32),
                        pltpu.SemaphoreType.DMA, pltpu.SemaphoreType.DMA],
        compiler_params=_cparams(("arbitrary",), 32),
        name="combine_norm",
    )(dest_flat, out_buf, x1, gate, g_final.reshape(1, d))


def _dispatch_kernel(zrow_ref, dest_hbm, h2_ref, xs_hbm, dsm, zbuf, sem_d, sem_z, sem_r, *, td):
    i = pl.program_id(0)
    block_rows = MOE_ROWS * TOKEN_TILE_ROWS

    def zero_copy(e):
        return pltpu.make_async_copy(zbuf, xs_hbm.at[pl.ds(_tile_start(zrow_ref[e]), block_rows), :], sem_z)

    def tail_copy(blk):
        return pltpu.make_async_copy(zbuf, xs_hbm.at[pl.ds(_tile_start(blk * MOE_ROWS), block_rows), :], sem_z)

    @pl.when(i == 0)
    def _clear_unwritten_blocks():
        zbuf[...] = jnp.zeros_like(zbuf)
        first_unused = zrow_ref[N_EXPERTS]
        n_blocks = xs_hbm.shape[0] // block_rows
        for e in range(N_EXPERTS):
            @pl.when(zrow_ref[e] >= 0)
            def _start(e=e):
                zero_copy(e).start()
        lax.fori_loop(first_unused, n_blocks, lambda blk, c: (tail_copy(blk).start(), c)[1], 0)
        for e in range(N_EXPERTS):
            @pl.when(zrow_ref[e] >= 0)
            def _wait(e=e):
                zero_copy(e).wait()
        lax.fori_loop(first_unused, n_blocks, lambda blk, c: (tail_copy(blk).wait(), c)[1], 0)

    slots = _slot_tile_copy(dest_hbm, dsm, sem_d, i, td)
    slots.start()
    slots.wait()

    def issue(r, carry):
        src = _token_rows(h2_ref, r)
        for k in range(TOP_K):
            pltpu.make_async_copy(src, _token_rows(xs_hbm, dsm[r * TOP_K + k]), sem_r).start(priority=k % 2)
        return carry

    lax.fori_loop(0, td, issue, 0, unroll=8)
    for _ in range(TOP_K):
        pltpu.make_async_copy(h2_ref, xs_hbm.at[pl.ds(0, td * TOKEN_TILE_ROWS), :], sem_r).wait()


def _tile_start(slot):
    return pl.multiple_of(slot * TOKEN_TILE_ROWS, TOKEN_TILE_ROWS)


def _dispatch(h2_tiles, dest_flat, zero_rows, n_slots):
    td = DISPATCH_TILE
    n = h2_tiles.shape[0] // TOKEN_TILE_ROWS
    assert n % td == 0
    grid_spec = pltpu.PrefetchScalarGridSpec(
        num_scalar_prefetch=1,
        grid=(n // td,),
        in_specs=[pl.BlockSpec(memory_space=pl.ANY),
                  pl.BlockSpec((td * TOKEN_TILE_ROWS, LANES), lambda i, zr: (i, 0))],
        out_specs=pl.BlockSpec(memory_space=pl.ANY),
        scratch_shapes=[pltpu.SMEM((td * TOP_K,), jnp.int32),
                        pltpu.VMEM((MOE_ROWS * TOKEN_TILE_ROWS, LANES), F32),
                        pltpu.SemaphoreType.DMA, pltpu.SemaphoreType.DMA, pltpu.SemaphoreType.DMA])
    return pl.pallas_call(
        functools.partial(_dispatch_kernel, td=td),
        grid_spec=grid_spec,
        out_shape=jax.ShapeDtypeStruct((n_slots * TOKEN_TILE_ROWS, LANES), F32),
        compiler_params=_cparams(("arbitrary",), 32),
        name="dispatch",
    )(zero_rows, dest_flat, h2_tiles)


def _route_kernel(lg_ref, eid_ref, gate_ref, rank_ref, cnt_ref, run_scr):
    i = pl.program_id(0)

    @pl.when(i == 0)
    def _init():
        run_scr[...] = jnp.zeros_like(run_scr)

    x = lg_ref[...]
    tm = x.shape[0]
    lane = lax.broadcasted_iota(jnp.int32, x.shape, 1).astype(F32)
    vals, hots = [], []
    eid = jnp.zeros(x.shape, F32)
    for k in range(TOP_K):
        m = jnp.max(x, axis=-1, keepdims=True)
        idx = jnp.min(jnp.where(x == m, lane, float(EXPERT_LANES)), axis=-1, keepdims=True)
        hot = lane == idx
        vals.append(m)
        hots.append(hot)
        eid = jnp.where(lane == float(k), idx, eid)
        x = jnp.where(hot, -jnp.inf, x)
    es = [jnp.exp(v - vals[0]) for v in vals]
    den = es[0]
    for e in es[1:]:
        den = den + e
    gate = jnp.zeros(x.shape, F32)
    for k in range(TOP_K):
        gate = jnp.where(lane == float(k), es[k] / den, gate)

    cnt_tok = hots[0].astype(F32)
    for hot in hots[1:]:
        cnt_tok = cnt_tok + hot.astype(F32)
    earlier = lax.broadcasted_iota(jnp.int32, (tm, tm), 1) < lax.broadcasted_iota(jnp.int32, (tm, tm), 0)
    before = run_scr[...] + jnp.dot(earlier.astype(BF16), cnt_tok.astype(BF16), preferred_element_type=F32)
    rank = jnp.zeros(x.shape, F32)
    for k in range(TOP_K):
        rank = jnp.where(lane == float(k), jnp.sum(jnp.where(hots[k], before, 0.0), axis=-1, keepdims=True), rank)
    run_scr[...] = run_scr[...] + jnp.sum(cnt_tok, axis=0, keepdims=True)

    eid_ref[...] = eid.astype(jnp.int32)
    gate_ref[...] = gate
    rank_ref[...] = rank.astype(jnp.int32)
    cnt_ref[...] = run_scr[...].astype(jnp.int32)


def _route(logits):
    n = logits.shape[0]
    tm = ROW_TILE
    row = pl.BlockSpec((tm, EXPERT_LANES), lambda i: (i, 0))
    eid, gate, rank, cnt = pl.pallas_call(
        _route_kernel,
        grid=(n // tm,),
        in_specs=[row],
        out_specs=[row, row, row, _const_spec((1, EXPERT_LANES))],
        out_shape=[jax.ShapeDtypeStruct((n, EXPERT_LANES), jnp.int32), jax.ShapeDtypeStruct((n, EXPERT_LANES), F32),
                   jax.ShapeDtypeStruct((n, EXPERT_LANES), jnp.int32),
                   jax.ShapeDtypeStruct((1, EXPERT_LANES), jnp.int32)],
        scratch_shapes=[pltpu.VMEM((1, EXPERT_LANES), F32)],
        compiler_params=_cparams(("arbitrary",), 32),
        name="route",
    )(logits)

    counts = cnt[0, :N_EXPERTS]
    padded = (counts + MOE_ROWS - 1) // MOE_ROWS * MOE_ROWS
    p_end = jnp.cumsum(padded)
    p_start = p_end - padded
    experts = jnp.arange(N_EXPERTS, dtype=jnp.int32)
    start_of = jnp.sum(jnp.where(eid[:, :TOP_K, None] == experts, p_start, 0), axis=-1)
    dest_flat = (start_of + rank[:, :TOP_K]).reshape(n * TOP_K).astype(jnp.int32)
    n_blocks = -(-(n * TOP_K + N_EXPERTS * (MOE_ROWS - 1)) // MOE_ROWS)
    block_start = jnp.arange(n_blocks, dtype=jnp.int32) * MOE_ROWS
    block_e = jnp.minimum(jnp.sum(p_end[None, :] <= block_start[:, None], axis=1), N_EXPERTS - 1).astype(jnp.int32)
    n_valid = (p_end[-1] // MOE_ROWS).astype(jnp.int32).reshape(1)
    block_e = jnp.where(jnp.arange(n_blocks) < n_valid[0], block_e, block_e[n_valid[0] - 1])
    zero_rows = jnp.concatenate([jnp.where(padded > 0, p_end - MOE_ROWS, -1), n_valid]).astype(jnp.int32)
    return gate, dest_flat, block_e, n_valid, zero_rows, n_blocks * MOE_ROWS


def _in_splits(dt, u_copy, v_layout, with_keys):
    off = np.cumsum([0, D_MODEL, Q_WIDTH, K_WIDTH, V_WIDTH, D_MODEL, N_BRANCH * D_MODEL])
    q_scale = HEAD_DIM ** -0.5 * math.log2(math.e)
    keys = ((int(off[2]), K_WIDTH, 1.0, ((F32, PLAIN),)),) if with_keys else ()
    return ((int(off[0]), D_MODEL, 1.0, (u_copy,)),
            (int(off[1]), Q_WIDTH, q_scale, ((dt, PLAIN),)),
            *keys,
            (int(off[3]), V_WIDTH, 1.0, ((F32, v_layout), (BF16, WITH_ONES))),
            (int(off[4]), D_MODEL, X_DIM ** -0.5, ((dt, PLAIN),)),
            (int(off[5]), N_BRANCH * D_MODEL, 1.0, ((F32, PLAIN),)))


def kernel(x_prompt, x_sample, mem_prompt, cache_k, cache_v, cache_mem_k, cache_mem_v, state_pool, page_table, g_mix, w_in, w_pool_group, pool_scale, lambda_q1, lambda_k1, lambda_q2, lambda_k2, g_subln, g_mem, w_mem_kv, w_pool_proj, w_diff_proj, w_cross_proj, w_o, g_ffn, w_router, b_router, w_up, b_up, w_down, b_down, g_final):
    b, s, d = x_prompt.shape
    db, t, _ = x_sample.shape
    depth = w_in.shape[0]
    assert depth == 1, "the step is written for the single-layer trunk this problem states"
    n_phys, page = cache_k.shape[1], cache_k.shape[2]
    n_mem = mem_prompt.shape[1]
    n_p, n_s = b * s, db * t

    cache_kt = cache_k.transpose(0, 1, 3, 4, 5, 2).reshape(depth * n_phys * K_WIDTH, page)
    cache_v2 = cache_v.reshape(depth * n_phys * page * N_KV_HEADS, V_DIM)
    cmk = _memory_rows(cache_mem_k)
    cmv = _memory_rows(cache_mem_v)

    xp = x_prompt.reshape(n_p, d)
    xs = x_sample.reshape(n_s, d)
    outs = {name: [] for name in ("kp", "vp", "mk", "mv", "hp", "ks", "vs", "hs")}

    for l in range(depth):
        lam_init = 0.8 - 0.6 * math.exp(-0.3 * l)
        lam_vecs = jnp.stack([lambda_q1[l], lambda_k1[l], lambda_q2[l], lambda_k2[l]]).astype(F32)
        w_in_bf = w_in[l].astype(BF16)
        wr = jnp.zeros((d, EXPERT_LANES), BF16).at[:, :N_EXPERTS].set(w_router[l].astype(BF16))
        br = jnp.full((1, EXPERT_LANES), MASK_VALUE, F32).at[0, :N_EXPERTS].set(b_router[l].astype(F32))
        wts = dict(wg=w_pool_group[l].astype(BF16), pool_scale=pool_scale[l].reshape(1, d).astype(F32),
                   wpp=w_pool_proj[l].astype(BF16), wdp=w_diff_proj[l].astype(BF16),
                   wcp=w_cross_proj[l].astype(BF16), wo=w_o[l].astype(BF16),
                   g_ffn=g_ffn[l].reshape(1, d).astype(F32), wr=wr, br=br)

        mk, mv = _rms_proj(mem_prompt.reshape(b * n_mem, d), g_mem[l], w_mem_kv[l].astype(BF16),
                           ((0, d, 1.0, ((F32, MEMORY_ROWS),)), (d, d, 1.0, ((F32, MEMORY_ROWS),))))

        k_off = D_MODEL + Q_WIDTH
        wk_t = w_in[l][:, k_off:k_off + K_WIDTH].T.astype(BF16)
        pooled_p, u_tail_p, q_p, v_p, v_pb, qx_p, gl_p, kt_p, kt_pb = _rms_proj(
            xp, g_mix[l], w_in_bf, _in_splits(BF16, (BF16, POOLED), VALUE_ROWS, with_keys=False), wt_bf=wk_t,
            transposed=(F32, BF16), tokens_per_seq=s)
        adiff_p = _attn_prompt(q_p.reshape(b, s, Q_WIDTH), kt_pb, v_pb.reshape(b, s, 2 * V_WIDTH), lam_vecs,
                               g_subln[l], lam_init).reshape(n_p, d)
        across_p = _cross(qx_p.reshape(b, s, d), mk, mv, n_mem, 0, BF16).reshape(n_p, d)

        u_s, q_s, k_s, v_s, _, qx_s, gl_s = _rms_proj(xs, g_mix[l], w_in_bf,
                                                           _in_splits(F32, (F32, PLAIN), PLAIN, with_keys=True))
        pooled_s = _pool_sample(state_pool[l], u_s.reshape(db, t, d)).reshape(n_s, d)
        adiff_s = _attn_sample(q_s.reshape(db, t, Q_WIDTH), k_s.reshape(db, t, K_WIDTH), v_s.reshape(db, t, V_WIDTH),
                               cache_kt, cache_v2, page, page_table + l * n_phys, lam_vecs, g_subln[l],
                               lam_init).reshape(n_s, d)
        across_s = _cross(qx_s.reshape(db, t, d), cmk, cmv, n_mem, l * db, F32).reshape(n_s, d)

        x1, h2, logits = _merge((pooled_p, adiff_p, across_p, gl_p, xp), (pooled_s, adiff_s, across_s, gl_s, xs), wts)
        gate, dest_flat, block_e, n_valid, zero_rows, n_slots = _route(logits)
        out_buf = _experts(_dispatch(h2, dest_flat, zero_rows, n_slots), block_e, n_valid, w_up[l],
                           b_up[l].reshape(N_EXPERTS, 1, 2 * D_FF), w_down[l], b_down[l].reshape(N_EXPERTS, 1, d))
        xp, xs = _combine(x1, out_buf, dest_flat, gate, g_final, n_p)

        outs["kp"].append(kt_p.reshape(b, N_KV_HEADS, 2, HEAD_DIM, s).transpose(0, 4, 1, 2, 3))
        outs["vp"].append(v_p.reshape(b, s, N_KV_HEADS, V_DIM))
        outs["mk"].append(_memory_from_rows(mk, 1, b, n_mem)[0])
        outs["mv"].append(_memory_from_rows(mv, 1, b, n_mem)[0])
        outs["hp"].append(u_tail_p[:, -POOL_STATE:].astype(state_pool.dtype))
        outs["ks"].append(k_s.reshape(db, t, N_KV_HEADS, 2, HEAD_DIM))
        outs["vs"].append(v_s.reshape(db, t, N_KV_HEADS, V_DIM))
        outs["hs"].append(jnp.concatenate([state_pool[l], u_s.reshape(db, t, d).astype(state_pool.dtype)],
                                          axis=1)[:, -POOL_STATE:])

    return (xp.reshape(b, s, d), xs.reshape(db, t, d), jnp.stack(outs["kp"]), jnp.stack(outs["vp"]),
            jnp.stack(outs["mk"]), jnp.stack(outs["mv"]), jnp.stack(outs["hp"]), jnp.stack(outs["ks"]),
            jnp.stack(outs["vs"]), jnp.stack(outs["hs"]))
```
